```python
import math
import jax, jax.numpy as jnp
from jax import lax
import numpy as np

D_MODEL = 1024
BATCH = 8
SEQ = 2048
DEPTH = 4
DEC_BATCH = 128
DEC_SEQ = 1
PAST_LEN = 8192
PAGE_SIZE = 128

F32 = jnp.float32
RET_HEADS = 4
RET_DK = 128
RET_DV = 128
RET_CHUNK = 128
ROPE_BASE = 10000.0
SSD_HEADS = 16
SSD_HEAD_DIM = 64
SSD_GROUPS = 2
SSD_STATE = 128
SSD_CONV = 4
SSD_CHUNK = 128
SSD_INNER = SSD_HEADS * SSD_HEAD_DIM
SSD_CONV_DIM = SSD_INNER + 2 * SSD_GROUPS * SSD_STATE
SWA_Q_HEADS = 8
SWA_KV_HEADS = 2
SWA_HEAD_DIM = 64
WINDOW = 128
SWA_QBLK = 128
N_BUCKETS = 32
BUCKET_MAX_DIST = 128
D_FF = ((8 * D_MODEL // 3 + 255) // 256) * 256
ALPHA = (2 * DEPTH) ** 0.25
BETA = (8 * DEPTH) ** -0.25
EPS = 1e-5
IN_SIZES = (RET_HEADS * RET_DK, RET_HEADS * RET_DK, RET_HEADS * RET_DV, RET_HEADS * RET_DV,
            SSD_INNER, SSD_CONV_DIM, SSD_HEADS,
            SWA_Q_HEADS * SWA_HEAD_DIM, SWA_KV_HEADS * SWA_HEAD_DIM, SWA_KV_HEADS * SWA_HEAD_DIM,
            3 * D_MODEL)
N_IN = sum(IN_SIZES)

kernel_name = 'hybrid_retention_ssd_swa_deepnorm_step'


def _offsets(sizes):
    out, t = [], 0
    for s in sizes[:-1]:
        t += s
        out.append(t)
    return out


def _layer_norm(x, g, b):
    xf = x.astype(F32)
    mu = jnp.mean(xf, -1, keepdims=True)
    var = jnp.mean(jnp.square(xf - mu), -1, keepdims=True)
    return ((xf - mu) * lax.rsqrt(var + EPS) * g.astype(F32) + b.astype(F32)).astype(x.dtype)


def _head_norm(o):
    mu = jnp.mean(o, -1, keepdims=True)
    var = jnp.mean(jnp.square(o - mu), -1, keepdims=True)
    return (o - mu) * lax.rsqrt(var + EPS)


def _group_rms(y, w):
    yg = y.reshape(y.shape[:-1] + (SSD_GROUPS, SSD_INNER // SSD_GROUPS))
    yg = yg * lax.rsqrt(jnp.mean(jnp.square(yg), -1, keepdims=True) + EPS)
    return yg.reshape(y.shape) * w.astype(F32)


def _rotary(x, pos):
    half = x.shape[-1] // 2
    inv = ROPE_BASE ** (-jnp.arange(half, dtype=F32) / half)
    ang = pos[:, None] * inv[None, :]
    cos = jnp.cos(ang)[None, :, None, :]
    sin = jnp.sin(ang)[None, :, None, :]
    xf = x.astype(F32)
    x1, x2 = xf[..., :half], xf[..., half:]
    return jnp.concatenate([x1 * cos - x2 * sin, x1 * sin + x2 * cos], -1)


def _retention(q, k, v, s0):
    bsz, L, H, dk = q.shape
    dv = v.shape[-1]
    c = RET_CHUNK if L % RET_CHUNK == 0 else L
    n = L // c
    lg = jnp.log(1.0 - 2.0 ** (-5.0 - jnp.arange(H, dtype=F32)))
    qc = q.reshape(bsz, n, c, H, dk)
    kc = k.reshape(bsz, n, c, H, dk)
    vc = v.reshape(bsz, n, c, H, dv).astype(F32)
    i = jnp.arange(c, dtype=F32)
    rel = i[:, None] - i[None, :]
    dmat = jnp.where(rel >= 0, jnp.exp(lg[:, None, None] * jnp.maximum(rel, 0.0)), 0.0)
    sc = jnp.einsum('bnihd,bnjhd->bnhij', qc, kc) * dmat
    o_intra = jnp.einsum('bnhij,bnjhv->bnihv', sc, vc)
    kdec = jnp.exp(lg[None, :] * (c - 1 - i)[:, None])
    kv = jnp.einsum('bnjhd,jh,bnjhv->nbhdv', kc, kdec, vc)
    cdec = jnp.exp(lg * c)[None, :, None, None]

    def step(s, kv_c):
        return cdec * s + kv_c, s

    s_fin, s_before = lax.scan(step, s0.astype(F32), kv)
    qdec = jnp.exp(lg[None, :] * (i + 1.0)[:, None])
    o_cross = jnp.einsum('bnihd,nbhdv,ih->bnihv', qc, s_before, qdec)
    return (o_intra + o_cross).reshape(bsz, L, H, dv), s_fin


def _causal_conv(xbc, buf, w, b):
    L = xbc.shape[1]
    ext = jnp.concatenate([buf.astype(xbc.dtype), xbc], axis=1)
    out = sum(w[t] * ext[:, t:t + L] for t in range(SSD_CONV)) + b
    return jax.nn.silu(out), ext[:, -(SSD_CONV - 1):]


def _ssd(x, dt, a, bm, cm, s0):
    bsz, L, H, P = x.shape
    G, N = bm.shape[2], bm.shape[3]
    hg = H // G
    c = SSD_CHUNK if L % SSD_CHUNK == 0 else L
    n = L // c
    xc = x.reshape(bsz, n, c, G, hg, P).astype(F32)
    dtc = dt.reshape(bsz, n, c, G, hg)
    bc = bm.reshape(bsz, n, c, G, N).astype(F32)
    cc = cm.reshape(bsz, n, c, G, N).astype(F32)
    cs = jnp.cumsum(dtc * a.reshape(G, hg), axis=2)
    seg = cs[:, :, :, None] - cs[:, :, None, :]
    causal = jnp.tril(jnp.ones((c, c), dtype=bool))[:, :, None, None]
    lmat = jnp.exp(jnp.where(causal, seg, -jnp.inf))
    cb = jnp.einsum('bcigs,bcjgs->bcijg', cc, bc)
    wmat = cb[..., None] * lmat * dtc[:, :, None]
    y_intra = jnp.einsum('bcijgh,bcjghp->bcighp', wmat, xc)
    dec_end = jnp.exp(cs[:, :, -1:] - cs)
    st = jnp.einsum('bcjgs,bcjgh,bcjghp->cbghps', bc, dec_end * dtc, xc)
    chunk_dec = jnp.exp(cs[:, :, -1]).transpose(1, 0, 2, 3)

    def step(s, inp):
        st_c, dec_c = inp
        return dec_c[..., None, None] * s + st_c, s

    s_fin, s_before = lax.scan(step, s0.astype(F32).reshape(bsz, G, hg, P, N), (st, chunk_dec))
    y_cross = jnp.einsum('bcigs,cbghps,bcigh->bcighp', cc, s_before, jnp.exp(cs))
    return (y_intra + y_cross).reshape(bsz, L, H, P), s_fin.reshape(bsz, H, P, N)


def _t5_bucket(dist):
    max_exact = N_BUCKETS // 2
    df = jnp.maximum(dist, 1).astype(F32)
    large = max_exact + (jnp.log(df / max_exact) / math.log(BUCKET_MAX_DIST / max_exact)
                         * (N_BUCKETS - max_exact)).astype(jnp.int32)
    large = jnp.minimum(large, N_BUCKETS - 1)
    return jnp.where(dist < max_exact, dist, large)


def _swa(q, k, v, kbuf, vbuf, buf_valid, sinks, rel_bias):
    bsz, L = q.shape[:2]
    G = SWA_Q_HEADS // SWA_KV_HEADS
    qs = SWA_QBLK if L % SWA_QBLK == 0 else L
    nb = L // qs
    k_ext = jnp.concatenate([kbuf.astype(k.dtype), k], axis=1)
    v_ext = jnp.concatenate([vbuf.astype(v.dtype), v], axis=1)
    kidx = (jnp.arange(nb) * qs)[:, None] + jnp.arange(qs + WINDOW)[None, :]
    kb = k_ext[:, kidx]
    vb = v_ext[:, kidx]
    qb = q.reshape(bsz, nb, qs, SWA_KV_HEADS, G, SWA_HEAD_DIM)
    s = jnp.einsum('bnqhgd,bnkhd->bnhgqk', qb, kb).astype(F32) * SWA_HEAD_DIM ** -0.5
    dist = jnp.arange(qs)[:, None] + WINDOW - jnp.arange(qs + WINDOW)[None, :]
    band = (dist >= 0) & (dist <= WINDOW)
    key_ok = (kidx >= WINDOW) | buf_valid
    mask = band[None] & key_ok[:, None, :]
    bias = rel_bias[_t5_bucket(jnp.maximum(dist, 0))].astype(F32)
    bias = bias.transpose(2, 0, 1).reshape(SWA_KV_HEADS, G, qs, qs + WINDOW)
    logits = jnp.where(mask[None, :, None, None], s + bias, -1e30)
    sink = jnp.broadcast_to(sinks.astype(F32).reshape(1, 1, SWA_KV_HEADS, G, 1, 1), logits.shape[:-1] + (1,))
    p = jax.nn.softmax(jnp.concatenate([logits, sink], axis=-1), axis=-1)[..., :-1]
    o = jnp.einsum('bnhgqk,bnkhd->bnqhgd', p, vb.astype(F32))
    o = o.reshape(bsz, L, SWA_Q_HEADS * SWA_HEAD_DIM).astype(q.dtype)
    return o, k_ext[:, -WINDOW:], v_ext[:, -WINDOW:]


def _layer(x, pos, s_ret, s_ssm, s_conv, kbuf, vbuf, buf_valid, rel_bias, p):
    (w_in, conv_w, conv_b, dt_bias, a_log, d_skip, ssd_norm_w, sinks,
     w_br_ret, w_br_ssd, w_br_swa, w_out, ln1_g, ln1_b, ln2_g, ln2_b,
     w_ffn_gate, w_ffn_up, w_ffn_down) = p
    bsz, L, _ = x.shape
    u = x @ w_in
    (q_r, k_r, v_r, g_r, z, xbc, dt_raw, q_c, k_c, v_c, gates) = jnp.split(u, _offsets(IN_SIZES), axis=-1)

    q_r = _rotary(q_r.reshape(bsz, L, RET_HEADS, RET_DK), pos)
    k_r = _rotary(k_r.reshape(bsz, L, RET_HEADS, RET_DK), pos) * RET_DK ** -0.5
    o_r, s_ret_new = _retention(q_r, k_r, v_r.reshape(bsz, L, RET_HEADS, RET_DV), s_ret)
    o_r = _head_norm(o_r).reshape(bsz, L, RET_HEADS * RET_DV) * jax.nn.silu(g_r.astype(F32))
    br_a = o_r.astype(x.dtype) @ w_br_ret

    xbc, s_conv_new = _causal_conv(xbc, s_conv, conv_w, conv_b)
    xs, bm, cm = jnp.split(xbc, [SSD_INNER, SSD_INNER + SSD_GROUPS * SSD_STATE], axis=-1)
    xs = xs.reshape(bsz, L, SSD_HEADS, SSD_HEAD_DIM)
    dt = jax.nn.softplus(dt_raw.astype(F32) + dt_bias.astype(F32))
    a = -jnp.exp(a_log.astype(F32))
    y, s_ssm_new = _ssd(xs, dt, a, bm.reshape(bsz, L, SSD_GROUPS, SSD_STATE),
                        cm.reshape(bsz, L, SSD_GROUPS, SSD_STATE), s_ssm)
    y = y + d_skip.astype(F32)[:, None] * xs.astype(F32)
    y = y.reshape(bsz, L, SSD_INNER) * jax.nn.silu(z.astype(F32))
    br_b = _group_rms(y, ssd_norm_w).astype(x.dtype) @ w_br_ssd

    o_c, kbuf_new, vbuf_new = _swa(q_c.reshape(bsz, L, SWA_Q_HEADS, SWA_HEAD_DIM),
                                   k_c.reshape(bsz, L, SWA_KV_HEADS, SWA_HEAD_DIM),
                                   v_c.reshape(bsz, L, SWA_KV_HEADS, SWA_HEAD_DIM),
                                   kbuf, vbuf, buf_valid, sinks, rel_bias)
    br_c = o_c @ w_br_swa

    g_a, g_b, g_c = jnp.split(jax.nn.sigmoid(gates), 3, axis=-1)
    mix = (g_a * br_a + g_b * br_b + g_c * br_c) @ w_out
    x = _layer_norm(ALPHA * x + mix, ln1_g, ln1_b)
    ffn = (jax.nn.silu(x @ w_ffn_gate) * (x @ w_ffn_up)) @ w_ffn_down
    x = _layer_norm(ALPHA * x + ffn, ln2_g, ln2_b)
    return x, (s_ret_new.astype(s_ret.dtype), s_ssm_new.astype(s_ssm.dtype),
               s_conv_new.astype(s_conv.dtype), kbuf_new.astype(kbuf.dtype), vbuf_new.astype(vbuf.dtype))


def _trunk(x, pos, states, buf_valid, rel_bias, params):
    new = [[], [], [], [], []]
    for l in range(DEPTH):
        p = tuple(w[l] for w in params)
        x, st = _layer(x, pos, states[0][l], states[1][l], states[2][l], states[3][l], states[4][l],
                       buf_valid, rel_bias, p)
        for j in range(5):
            new[j].append(st[j])
    return x, [jnp.stack(s) for s in new]


def setup_inputs(seed: int = 0) -> dict:
    key = jax.random.key(seed)
    ks = jax.random.split(key, 32)

    def nrm(k, shape, scale):
        return jax.random.normal(k, shape, F32) * scale

    dt0 = jnp.exp(jax.random.uniform(ks[10], (DEPTH, SSD_HEADS), F32, math.log(1e-3), math.log(1e-1)))
    return {
        'x_prompt': nrm(ks[0], (BATCH, SEQ, D_MODEL), 1.0),
        'x_sample': nrm(ks[1], (DEC_BATCH, DEC_SEQ, D_MODEL), 1.0),
        'state_ret': nrm(ks[2], (DEPTH, DEC_BATCH, RET_HEADS, RET_DK, RET_DV), 1.0),
        'state_ssm': nrm(ks[3], (DEPTH, DEC_BATCH, SSD_HEADS, SSD_HEAD_DIM, SSD_STATE), 0.5),
        'state_conv': nrm(ks[4], (DEPTH, DEC_BATCH, SSD_CONV - 1, SSD_CONV_DIM), 1.0),
        'cache_swa_k': nrm(ks[5], (DEPTH, DEC_BATCH, WINDOW, SWA_KV_HEADS, SWA_HEAD_DIM), 1.0),
        'cache_swa_v': nrm(ks[6], (DEPTH, DEC_BATCH, WINDOW, SWA_KV_HEADS, SWA_HEAD_DIM), 1.0),
        'w_in': nrm(ks[7], (DEPTH, D_MODEL, N_IN), D_MODEL ** -0.5),
        'conv_w': nrm(ks[8], (DEPTH, SSD_CONV, SSD_CONV_DIM), SSD_CONV ** -0.5),
        'conv_b': nrm(ks[9], (DEPTH, SSD_CONV_DIM), 0.02),
        'dt_bias': dt0 + jnp.log(-jnp.expm1(-dt0)),
        'a_log': jnp.log(jax.random.uniform(ks[11], (DEPTH, SSD_HEADS), F32, 1.0, 16.0)),
        'd_skip': 1.0 + nrm(ks[12], (DEPTH, SSD_HEADS), 0.1),
        'ssd_norm_w': 1.0 + nrm(ks[13], (DEPTH, SSD_INNER), 0.05),
        'sinks': nrm(ks[14], (DEPTH, SWA_Q_HEADS), 0.5),
        'rel_bias': nrm(ks[15], (N_BUCKETS, SWA_Q_HEADS), 0.2),
        'w_br_ret': nrm(ks[16], (DEPTH, RET_HEADS * RET_DV, D_MODEL), (RET_HEADS * RET_DV) ** -0.5 * BETA),
        'w_br_ssd': nrm(ks[17], (DEPTH, SSD_INNER, D_MODEL), SSD_INNER ** -0.5 * BETA),
        'w_br_swa': nrm(ks[18], (DEPTH, SWA_Q_HEADS * SWA_HEAD_DIM, D_MODEL), (SWA_Q_HEADS * SWA_HEAD_DIM) ** -0.5 * BETA),
        'w_out': nrm(ks[19], (DEPTH, D_MODEL, D_MODEL), D_MODEL ** -0.5 * BETA),
        'ln1_g': 1.0 + nrm(ks[20], (DEPTH, D_MODEL), 0.05),
        'ln1_b': nrm(ks[21], (DEPTH, D_MODEL), 0.02),
        'ln2_g': 1.0 + nrm(ks[22], (DEPTH, D_MODEL), 0.05),
        'ln2_b': nrm(ks[23], (DEPTH, D_MODEL), 0.02),
        'w_ffn_gate': nrm(ks[24], (DEPTH, D_MODEL, D_FF), D_MODEL ** -0.5),
        'w_ffn_up': nrm(ks[25], (DEPTH, D_MODEL, D_FF), D_MODEL ** -0.5 * BETA),
        'w_ffn_down': nrm(ks[26], (DEPTH, D_FF, D_MODEL), D_FF ** -0.5 * BETA),
    }


def reference(x_prompt, x_sample, state_ret, state_ssm, state_conv, cache_swa_k, cache_swa_v,
              w_in, conv_w, conv_b, dt_bias, a_log, d_skip, ssd_norm_w, sinks, rel_bias,
              w_br_ret, w_br_ssd, w_br_swa, w_out, ln1_g, ln1_b, ln2_g, ln2_b,
              w_ffn_gate, w_ffn_up, w_ffn_down):
    params = (w_in, conv_w, conv_b, dt_bias, a_log, d_skip, ssd_norm_w, sinks,
              w_br_ret, w_br_ssd, w_br_swa, w_out, ln1_g, ln1_b, ln2_g, ln2_b,
              w_ffn_gate, w_ffn_up, w_ffn_down)
    bsz, dtp = x_prompt.shape[0], x_prompt.dtype
    empty = (jnp.zeros((DEPTH, bsz) + state_ret.shape[2:], dtp),
             jnp.zeros((DEPTH, bsz) + state_ssm.shape[2:], dtp),
             jnp.zeros((DEPTH, bsz) + state_conv.shape[2:], dtp),
             jnp.zeros((DEPTH, bsz) + cache_swa_k.shape[2:], dtp),
             jnp.zeros((DEPTH, bsz) + cache_swa_v.shape[2:], dtp))
    pos_p = jnp.arange(x_prompt.shape[1], dtype=F32)
    y_prompt, st_p = _trunk(x_prompt, pos_p, empty, False, rel_bias, params)
    ret_p, ssm_p, conv_p, k_p, v_p = st_p
    pos_s = PAST_LEN + jnp.arange(x_sample.shape[1], dtype=F32)
    y_sample, st_s = _trunk(x_sample, pos_s, (state_ret, state_ssm, state_conv, cache_swa_k, cache_swa_v),
                            True, rel_bias, params)
    ret_s, ssm_s, conv_s, k_s, v_s = st_s
    return (y_prompt, y_sample, ret_p, ssm_p, conv_p, k_p, v_p, ret_s, ssm_s, conv_s, k_s, v_s)
```

```python
import functools
import math

import jax
import jax.numpy as jnp
import numpy as np
from jax import lax
from jax.experimental import pallas as pl
from jax.experimental.pallas import tpu as pltpu

F32 = jnp.float32
BF16 = jnp.bfloat16

D_MODEL = 1024
DEPTH = 4
PAST_LEN = 8192
RET_HEADS = 4
RET_DK = 128
RET_DV = 128
ROPE_BASE = 10000.0
SSD_HEADS = 16
SSD_HEAD_DIM = 64
SSD_GROUPS = 2
SSD_STATE = 128
SSD_CONV = 4
SSD_INNER = SSD_HEADS * SSD_HEAD_DIM
SSD_CONV_DIM = SSD_INNER + 2 * SSD_GROUPS * SSD_STATE
SWA_Q_HEADS = 8
SWA_KV_HEADS = 2
SWA_HEAD_DIM = 64
WINDOW = 128
N_BUCKETS = 32
BUCKET_MAX_DIST = 128
D_FF = ((8 * D_MODEL // 3 + 255) // 256) * 256
ALPHA = (2 * DEPTH) ** 0.25
EPS = 1e-5
CHUNK = 128

RET_W = RET_HEADS * RET_DK
SWA_QW = SWA_Q_HEADS * SWA_HEAD_DIM
SWA_KW = SWA_KV_HEADS * SWA_HEAD_DIM

OFF_GATES = 0
OFF_QR = 3 * D_MODEL
OFF_KR = OFF_QR + RET_W
OFF_VR = OFF_KR + RET_W
OFF_GR = OFF_VR + RET_W
OFF_Z = OFF_GR + RET_W
OFF_XBC = OFF_Z + SSD_INNER
OFF_QC = OFF_XBC + SSD_CONV_DIM
OFF_KC = OFF_QC + SWA_QW
OFF_VC = OFF_KC + SWA_KW
OFF_DT = OFF_VC + SWA_KW
LANE = 128
PROJ_TN = 512
N_PROJ = ((OFF_DT + LANE + PROJ_TN - 1) // PROJ_TN) * PROJ_TN

VMEM_LIMIT = 48 * 1024 * 1024
NEG_BIG = -1e30


def _sigmoid(x):
    return 1.0 / (1.0 + jnp.exp(-x))


def _silu(x):
    return x * _sigmoid(x)


def _softplus(x):
    return jnp.maximum(x, 0.0) + jnp.log1p(jnp.exp(-jnp.abs(x)))


def _dot(a, b):
    return jnp.dot(a, b, preferred_element_type=F32)


def _dot_nt(a, b):
    return lax.dot_general(a, b, (((1,), (1,)), ((), ())), preferred_element_type=F32)


def _dot_tn(a, b):
    return lax.dot_general(a, b, (((0,), (0,)), ((), ())), preferred_element_type=F32)


def _split2(x):
    hi = x.astype(BF16)
    lo = (x - hi.astype(F32)).astype(BF16)
    return hi, lo


def _split3(x):
    hi = x.astype(BF16)
    r = x - hi.astype(F32)
    mid = r.astype(BF16)
    lo = (r - mid.astype(F32)).astype(BF16)
    return hi, mid, lo


def _params(sem):
    return pltpu.CompilerParams(dimension_semantics=sem, vmem_limit_bytes=VMEM_LIMIT)


def _mm_kernel(x_ref, w_ref, o_ref):
    o_ref[...] = _dot(x_ref[...], w_ref[...]).astype(o_ref.dtype)


def _matmul(x, w, tm, tn):
    m, k = x.shape
    n = w.shape[1]
    return pl.pallas_call(
        _mm_kernel,
        grid=(m // tm, n // tn),
        in_specs=[pl.BlockSpec((tm, k), lambda i, j: (i, 0)),
                  pl.BlockSpec((k, tn), lambda i, j: (0, j))],
        out_specs=pl.BlockSpec((tm, tn), lambda i, j: (i, j)),
        out_shape=jax.ShapeDtypeStruct((m, n), F32),
        compiler_params=_params(("parallel", "arbitrary")),
        name="in_proj",
    )(x, w)


def _rotary(x, cos, sin):
    return x * cos + pltpu.roll(x, RET_DK // 2, 1) * sin


def _head_norm_gate(o, g):
    mu = jnp.mean(o, axis=-1, keepdims=True)
    d = o - mu
    var = jnp.mean(d * d, axis=-1, keepdims=True)
    return d * lax.rsqrt(var + EPS) * _silu(g)


def _ret_prompt_kernel(q_ref, k_ref, v_ref, g_ref, cos_ref, sin_ref, dmat_ref, kdec_ref, qdec_ref,
                       o_ref, sfin_ref, s_scr, *, cdec):
    c = pl.program_id(1)

    @pl.when(c == 0)
    def _():
        s_scr[...] = jnp.zeros_like(s_scr)

    cos = cos_ref[...]
    sin = sin_ref[...]
    for h in range(RET_HEADS):
        sl = slice(h * RET_DK, (h + 1) * RET_DK)
        qr = _rotary(q_ref[:, sl], cos, sin)
        kr = _rotary(k_ref[:, sl], cos, sin) * (RET_DK ** -0.5)
        qb = qr.astype(BF16)
        vb = v_ref[:, sl].astype(BF16)
        sc = _dot_nt(qb, kr.astype(BF16)) * dmat_ref[h]
        s_prev = s_scr[h]
        o = _dot(sc.astype(BF16), vb) + _dot(qb, s_prev.astype(BF16)) * qdec_ref[:, sl]
        kd = (kr * kdec_ref[:, sl]).astype(BF16)
        s_scr[h] = cdec[h] * s_prev + _dot_tn(kd, vb)
        o_ref[:, sl] = _head_norm_gate(o, g_ref[:, sl]).astype(o_ref.dtype)

    @pl.when(c == pl.num_programs(1) - 1)
    def _():
        sfin_ref[0] = s_scr[...]


def _ret_consts(c):
    lg = [math.log(1.0 - 2.0 ** (-5.0 - h)) for h in range(RET_HEADS)]
    i = np.arange(c, dtype=np.float64)
    rel = i[:, None] - i[None, :]
    dmat = np.stack([np.where(rel >= 0, np.exp(l * np.maximum(rel, 0.0)), 0.0) for l in lg])
    kdec = np.concatenate([np.repeat(np.exp(l * (c - 1 - i))[:, None], RET_DK, 1) for l in lg], 1)
    qdec = np.concatenate([np.repeat(np.exp(l * (i + 1.0))[:, None], RET_DK, 1) for l in lg], 1)
    cdec = tuple(math.exp(l * c) for l in lg)
    return (jnp.asarray(dmat, F32), jnp.asarray(kdec, F32), jnp.asarray(qdec, F32), cdec)


def _ret_prompt(u, cos, sin, bsz, seq):
    nc = seq // CHUNK
    dmat, kdec, qdec, cdec = _ret_consts(CHUNK)
    row = lambda b, c: b * nc + c
    col = lambda off: off // RET_W
    usec = lambda off: pl.BlockSpec((CHUNK, RET_W), lambda b, c: (row(b, c), col(off)))
    const2 = lambda shape: pl.BlockSpec(shape, lambda b, c: (0,) * len(shape))
    return pl.pallas_call(
        functools.partial(_ret_prompt_kernel, cdec=cdec),
        grid=(bsz, nc),
        in_specs=[usec(OFF_QR), usec(OFF_KR), usec(OFF_VR), usec(OFF_GR),
                  pl.BlockSpec((CHUNK, RET_DK), lambda b, c: (c, 0)),
                  pl.BlockSpec((CHUNK, RET_DK), lambda b, c: (c, 0)),
                  const2((RET_HEADS, CHUNK, CHUNK)), const2((CHUNK, RET_W)), const2((CHUNK, RET_W))],
        out_specs=[pl.BlockSpec((CHUNK, RET_W), lambda b, c: (row(b, c), 0)),
                   pl.BlockSpec((1, RET_HEADS, RET_DK, RET_DV), lambda b, c: (b, 0, 0, 0))],
        out_shape=[jax.ShapeDtypeStruct((bsz * seq, RET_W), BF16),
                   jax.ShapeDtypeStruct((bsz, RET_HEADS, RET_DK, RET_DV), F32)],
        scratch_shapes=[pltpu.VMEM((RET_HEADS, RET_DK, RET_DV), F32)],
        compiler_params=_params(("arbitrary", "arbitrary")),
        name="ret_prompt",
    )(u, u, u, u, cos, sin, dmat, kdec, qdec)


def _expand_heads(v, e_ref):
    hi, lo = _split2(v)
    e = e_ref[...]
    return _dot(hi, e) + _dot(lo, e)


def _group_rms_gate(y, z, w):
    y = y * _silu(z)
    half = SSD_INNER // SSD_GROUPS
    outs = []
    for g in range(SSD_GROUPS):
        yg = y[:, g * half:(g + 1) * half]
        ms = jnp.mean(yg * yg, axis=-1, keepdims=True)
        outs.append(yg * lax.rsqrt(ms + EPS))
    return jnp.concatenate(outs, axis=-1) * w


def _ssd_prompt_kernel(xbc_ref, z_ref, dt_ref, cw_ref, cb_ref, dtb_ref, alog_ref, dskip_ref, nw_ref,
                       tril_ref, e_ref, y_ref, sfin_ref, ext_scr, st_scr, yi_scr):
    c = pl.program_id(1)
    pad = 8

    @pl.when(c == 0)
    def _():
        ext_scr[0:pad, :] = jnp.zeros((pad, SSD_CONV_DIM), F32)
        st_scr[...] = jnp.zeros_like(st_scr)

    @pl.when(c > 0)
    def _():
        ext_scr[0:pad, :] = ext_scr[CHUNK:CHUNK + pad, :]

    ext_scr[pad:pad + CHUNK, :] = xbc_ref[...]
    conv = cb_ref[...] + sum(
        cw_ref[t:t + 1, :] * ext_scr[pad - (SSD_CONV - 1) + t:pad - (SSD_CONV - 1) + t + CHUNK, :]
        for t in range(SSD_CONV))
    xact = _silu(conv)
    xs = xact[:, :SSD_INNER]
    xs_b = xs.astype(BF16)

    dt = _softplus(dt_ref[...] + dtb_ref[...])
    a = -jnp.exp(alog_ref[...])
    da = dt * a
    tril = tril_ref[...]
    d1, d2, d3 = _split3(da)
    cs = _dot(tril, d1) + _dot(tril, d2) + _dot(tril, d3)
    cs_last = cs[CHUNK - 1:CHUNK, :]
    cs_t = cs.T
    dt_t = dt.T
    exp_cs = _expand_heads(jnp.exp(cs), e_ref)
    xw = (xs * _expand_heads(jnp.exp(cs_last - cs) * dt, e_ref)).astype(BF16)

    ri = lax.broadcasted_iota(jnp.int32, (CHUNK, CHUNK), 0)
    ci = lax.broadcasted_iota(jnp.int32, (CHUNK, CHUNK), 1)
    causal = ri >= ci
    hg = SSD_HEADS // SSD_GROUPS
    gw = hg * SSD_HEAD_DIM
    for g in range(SSD_GROUPS):
        bg = xact[:, SSD_INNER + g * SSD_STATE:SSD_INNER + (g + 1) * SSD_STATE].astype(BF16)
        cg = xact[:, SSD_INNER + (SSD_GROUPS + g) * SSD_STATE:
                  SSD_INNER + (SSD_GROUPS + g + 1) * SSD_STATE].astype(BF16)
        cb = _dot_nt(cg, bg)
        for hh in range(hg):
            h = g * hg + hh
            seg = cs[:, h:h + 1] - cs_t[h:h + 1, :]
            lmat = jnp.exp(jnp.where(causal, seg, NEG_BIG))
            w = (cb * lmat * dt_t[h:h + 1, :]).astype(BF16)
            hs = slice(h * SSD_HEAD_DIM, (h + 1) * SSD_HEAD_DIM)
            yi_scr[:, hs] = _dot(w, xs_b[:, hs])
        gs = slice(g * gw, (g + 1) * gw)
        st_prev = st_scr[:, gs]
        yi_scr[:, gs] = yi_scr[:, gs] + _dot(cg, st_prev.astype(BF16)) * exp_cs[:, gs]
        st_scr[:, gs] = st_prev * exp_cs[CHUNK - 1:CHUNK, gs] + _dot_tn(bg, xw[:, gs])

    y = yi_scr[...] + dskip_ref[...] * xs
    y_ref[...] = _group_rms_gate(y, z_ref[...], nw_ref[...]).astype(y_ref.dtype)

    @pl.when(c == pl.num_programs(1) - 1)
    def _():
        for t in range(SSD_INNER // LANE):
            sfin_ref[0, t * LANE:(t + 1) * LANE, :] = st_scr[:, t * LANE:(t + 1) * LANE].T


def _ssd_consts():
    tril = np.tril(np.ones((CHUNK, CHUNK), np.float32))
    e = np.zeros((LANE, SSD_INNER), np.float32)
    for h in range(SSD_HEADS):
        e[h, h * SSD_HEAD_DIM:(h + 1) * SSD_HEAD_DIM] = 1.0
    return jnp.asarray(tril, BF16), jnp.asarray(e, BF16)


def _pad_lanes(v):
    return jnp.pad(v.astype(F32), (0, LANE - v.shape[0]))[None, :]


def _ssd_prompt(u, lp, bsz, seq):
    nc = seq // CHUNK
    tril, e = _ssd_consts()
    row = lambda b, c: b * nc + c
    const2 = lambda shape: pl.BlockSpec(shape, lambda b, c: (0,) * len(shape))
    return pl.pallas_call(
        _ssd_prompt_kernel,
        grid=(bsz, nc),
        in_specs=[pl.BlockSpec((CHUNK, SSD_CONV_DIM), lambda b, c: (row(b, c), OFF_XBC // SSD_CONV_DIM)),
                  pl.BlockSpec((CHUNK, SSD_INNER), lambda b, c: (row(b, c), OFF_Z // SSD_INNER)),
                  pl.BlockSpec((CHUNK, LANE), lambda b, c: (row(b, c), OFF_DT // LANE)),
                  const2((SSD_CONV, SSD_CONV_DIM)), const2((1, SSD_CONV_DIM)),
                  const2((1, LANE)), const2((1, LANE)), const2((1, SSD_INNER)), const2((1, SSD_INNER)),
                  const2((CHUNK, CHUNK)), const2((LANE, SSD_INNER))],
        out_specs=[pl.BlockSpec((CHUNK, SSD_INNER), lambda b, c: (row(b, c), 0)),
                   pl.BlockSpec((1, SSD_INNER, SSD_STATE), lambda b, c: (b, 0, 0))],
        out_shape=[jax.ShapeDtypeStruct((bsz * seq, SSD_INNER), BF16),
                   jax.ShapeDtypeStruct((bsz, SSD_INNER, SSD_STATE), F32)],
        scratch_shapes=[pltpu.VMEM((CHUNK + 8, SSD_CONV_DIM), F32),
                        pltpu.VMEM((SSD_STATE, SSD_INNER), F32),
                        pltpu.VMEM((CHUNK, SSD_INNER), F32)],
        compiler_params=_params(("arbitrary", "arbitrary")),
        name="ssd_prompt",
    )(u, u, u, lp["conv_w"], lp["conv_b"], lp["dt_bias"], lp["a_log"], lp["d_skip"], lp["norm_w"], tril, e)


def _t5_bucket_np(dist):
    max_exact = N_BUCKETS // 2
    df = np.maximum(dist, 1).astype(np.float32)
    large = max_exact + (np.log(df / np.float32(max_exact)) / np.float32(math.log(BUCKET_MAX_DIST / max_exact))
                         * np.float32(N_BUCKETS - max_exact)).astype(np.int32)
    large = np.minimum(large, N_BUCKETS - 1)
    return np.where(dist < max_exact, dist, large).astype(np.int32)


def _build_bias(rel_ref, idx, nheads_axis_first_shape):
    out = []
    for h in range(SWA_Q_HEADS):
        acc = jnp.zeros(nheads_axis_first_shape, F32)
        for b in range(N_BUCKETS):
            acc = jnp.where(idx == b, rel_ref[b, h], acc)
        out.append(acc)
    return out


def _swa_prompt_kernel(rel_ref, sink_ref, q_ref, kp_ref, kc_ref, vp_ref, vc_ref, idx_ref, o_ref, bias_scr):
    b = pl.program_id(0)
    n = pl.program_id(1)

    @pl.when(jnp.logical_and(b == 0, n == 0))
    def _():
        for h, bias in enumerate(_build_bias(rel_ref, idx_ref[...], (CHUNK, 2 * CHUNK))):
            bias_scr[h] = bias

    qi = lax.broadcasted_iota(jnp.int32, (CHUNK, 2 * CHUNK), 0)
    kj = lax.broadcasted_iota(jnp.int32, (CHUNK, 2 * CHUNK), 1)
    dist = qi + WINDOW - kj
    band = jnp.logical_and(dist >= 0, dist <= WINDOW)
    key_ok = jnp.logical_or(kj >= WINDOW, n > 0)
    mask = jnp.logical_and(band, key_ok)
    kk = jnp.concatenate([kp_ref[...], kc_ref[...]], axis=0).astype(BF16)
    vv = jnp.concatenate([vp_ref[...], vc_ref[...]], axis=0).astype(BF16)
    grp = SWA_Q_HEADS // SWA_KV_HEADS
    for h in range(SWA_Q_HEADS):
        kv = h // grp
        ks = slice(kv * SWA_HEAD_DIM, (kv + 1) * SWA_HEAD_DIM)
        qh = q_ref[:, h * SWA_HEAD_DIM:(h + 1) * SWA_HEAD_DIM].astype(BF16)
        s = _dot_nt(qh, kk[:, ks]) * (SWA_HEAD_DIM ** -0.5) + bias_scr[h]
        logits = jnp.where(mask, s, NEG_BIG)
        sink = sink_ref[h]
        m = jnp.maximum(jnp.max(logits, axis=-1, keepdims=True), sink)
        e = jnp.exp(logits - m)
        den = jnp.sum(e, axis=-1, keepdims=True) + jnp.exp(sink - m)
        p = (e / den).astype(BF16)
        o_ref[:, h * SWA_HEAD_DIM:(h + 1) * SWA_HEAD_DIM] = _dot(p, vv[:, ks]).astype(o_ref.dtype)


def _swa_prompt(u, rel_bias, sinks, bsz, seq):
    nb = seq // CHUNK
    qi = np.arange(CHUNK)[:, None]
    kj = np.arange(2 * CHUNK)[None, :]
    idx = jnp.asarray(_t5_bucket_np(np.maximum(qi + WINDOW - kj, 0)))
    row = lambda b, n: b * nb + n
    prev = lambda b, n: b * nb + jnp.maximum(n - 1, 0)
    smem = pl.BlockSpec(memory_space=pltpu.SMEM)
    return pl.pallas_call(
        _swa_prompt_kernel,
        grid=(bsz, nb),
        in_specs=[smem, smem,
                  pl.BlockSpec((CHUNK, SWA_QW), lambda b, n: (row(b, n), OFF_QC // SWA_QW)),
                  pl.BlockSpec((CHUNK, SWA_KW), lambda b, n: (prev(b, n), OFF_KC // SWA_KW)),
                  pl.BlockSpec((CHUNK, SWA_KW), lambda b, n: (row(b, n), OFF_KC // SWA_KW)),
                  pl.BlockSpec((CHUNK, SWA_KW), lambda b, n: (prev(b, n), OFF_VC // SWA_KW)),
                  pl.BlockSpec((CHUNK, SWA_KW), lambda b, n: (row(b, n), OFF_VC // SWA_KW)),
                  pl.BlockSpec((CHUNK, 2 * CHUNK), lambda b, n: (0, 0))],
        out_specs=pl.BlockSpec((CHUNK, SWA_QW), lambda b, n: (row(b, n), 0)),
        out_shape=jax.ShapeDtypeStruct((bsz * seq, SWA_QW), BF16),
        scratch_shapes=[pltpu.VMEM((SWA_Q_HEADS, CHUNK, 2 * CHUNK), F32)],
        compiler_params=_params(("arbitrary", "arbitrary")),
        name="swa_prompt",
    )(rel_bias, sinks, u, u, u, u, u, idx)


def _layer_norm(r, g, b):
    mu = jnp.mean(r, axis=-1, keepdims=True)
    d = r - mu
    var = jnp.mean(d * d, axis=-1, keepdims=True)
    return d * lax.rsqrt(var + EPS) * g + b


def _merge_kernel(or_ref, y_ref, oc_ref, ga_ref, gb_ref, gc_ref, x_ref, wa_ref, wb_ref, wc_ref, wo_ref,
                  g_ref, b_ref, xo_ref, xob_ref):
    mix = (_sigmoid(ga_ref[...]) * _dot(or_ref[...], wa_ref[...])
           + _sigmoid(gb_ref[...]) * _dot(y_ref[...], wb_ref[...])
           + _sigmoid(gc_ref[...]) * _dot(oc_ref[...], wc_ref[...]))
    r = ALPHA * x_ref[...] + _dot(mix.astype(BF16), wo_ref[...])
    xn = _layer_norm(r, g_ref[...], b_ref[...])
    xo_ref[...] = xn
    xob_ref[...] = xn.astype(BF16)


def _merge(o_r, y, o_c, u, x, lp, tm):
    m = x.shape[0]
    rows = lambda w: pl.BlockSpec((tm, w), lambda i: (i, 0))
    gate = lambda k: pl.BlockSpec((tm, D_MODEL), lambda i: (i, OFF_GATES // D_MODEL + k))
    const = lambda shape: pl.BlockSpec(shape, lambda i: (0, 0), pipeline_mode=pl.Buffered(1))
    return pl.pallas_call(
        _merge_kernel,
        grid=(m // tm,),
        in_specs=[rows(RET_W), rows(SSD_INNER), rows(SWA_QW), gate(0), gate(1), gate(2), rows(D_MODEL),
                  const((RET_W, D_MODEL)), const((SSD_INNER, D_MODEL)), const((SWA_QW, D_MODEL)),
                  const((D_MODEL, D_MODEL)), const((1, D_MODEL)), const((1, D_MODEL))],
        out_specs=[rows(D_MODEL), rows(D_MODEL)],
        out_shape=[jax.ShapeDtypeStruct((m, D_MODEL), F32), jax.ShapeDtypeStruct((m, D_MODEL), BF16)],
        compiler_params=_params(("parallel",)),
        name="merge",
    )(o_r, y, o_c, u, u, u, x, lp["w_br_ret"], lp["w_br_ssd"], lp["w_br_swa"], lp["w_out"],
      lp["ln1_g"], lp["ln1_b"])


def _ffn_kernel(xb_ref, x_ref, wg_ref, wu_ref, wd_ref, g_ref, b_ref, xo_ref, xob_ref):
    xb = xb_ref[...]
    hmid = _silu(_dot(xb, wg_ref[...])) * _dot(xb, wu_ref[...])
    r = ALPHA * x_ref[...] + _dot(hmid.astype(BF16), wd_ref[...])
    xn = _layer_norm(r, g_ref[...], b_ref[...])
    xo_ref[...] = xn
    xob_ref[...] = xn.astype(BF16)


def _ffn(xb, x, lp, tm):
    m = x.shape[0]
    rows = pl.BlockSpec((tm, D_MODEL), lambda i: (i, 0))
    const = lambda shape: pl.BlockSpec(shape, lambda i: (0, 0), pipeline_mode=pl.Buffered(1))
    return pl.pallas_call(
        _ffn_kernel,
        grid=(m // tm,),
        in_specs=[rows, rows, const((D_MODEL, D_FF)), const((D_MODEL, D_FF)), const((D_FF, D_MODEL)),
                  const((1, D_MODEL)), const((1, D_MODEL))],
        out_specs=[rows, rows],
        out_shape=[jax.ShapeDtypeStruct((m, D_MODEL), F32), jax.ShapeDtypeStruct((m, D_MODEL), BF16)],
        compiler_params=_params(("parallel",)),
        name="ffn",
    )(xb, x, lp["w_ffn_gate"], lp["w_ffn_up"], lp["w_ffn_down"], lp["ln2_g"], lp["ln2_b"])


DEC_BB = 8


def _row_diag(row, eye):
    n = row.shape[1]
    return jnp.where(eye, jnp.broadcast_to(row, (n, n)), 0.0).astype(BF16)


def _ret_dec_kernel(q_ref, k_ref, v_ref, g_ref, cos_ref, sin_ref, s_ref, o_ref, so_ref, *, gamma):
    eye = (lax.broadcasted_iota(jnp.int32, (RET_DK, RET_DK), 0)
           == lax.broadcasted_iota(jnp.int32, (RET_DK, RET_DK), 1))
    cos = cos_ref[...]
    sin = sin_ref[...]
    for h in range(RET_HEADS):
        sl = slice(h * RET_DK, (h + 1) * RET_DK)
        qr = _rotary(q_ref[:, sl], cos, sin)
        kr = _rotary(k_ref[:, sl], cos, sin) * (RET_DK ** -0.5)
        v = v_ref[:, sl]
        rows = []
        for bb in range(DEC_BB):
            kv = _dot(_row_diag(kr[bb:bb + 1, :], eye),
                      jnp.broadcast_to(v[bb:bb + 1, :], (RET_DK, RET_DV)).astype(BF16))
            s_new = gamma[h] * s_ref[bb, h] + kv
            so_ref[bb, h] = s_new
            q8 = jnp.broadcast_to(qr[bb:bb + 1, :], (8, RET_DK)).astype(BF16)
            rows.append(_dot(q8, s_new.astype(BF16))[0:1, :])
        o = jnp.concatenate(rows, axis=0)
        o_ref[:, sl] = _head_norm_gate(o, g_ref[:, sl])


def _ret_dec(u, cos, sin, s0):
    bsz = u.shape[0]
    gamma = tuple(1.0 - 2.0 ** (-5.0 - h) for h in range(RET_HEADS))
    usec = lambda off: pl.BlockSpec((DEC_BB, RET_W), lambda i: (i, off // RET_W))
    st = pl.BlockSpec((DEC_BB, RET_HEADS, RET_DK, RET_DV), lambda i: (i, 0, 0, 0))
    rot = pl.BlockSpec((1, RET_DK), lambda i: (0, 0))
    return pl.pallas_call(
        functools.partial(_ret_dec_kernel, gamma=gamma),
        grid=(bsz // DEC_BB,),
        in_specs=[usec(OFF_QR), usec(OFF_KR), usec(OFF_VR), usec(OFF_GR), rot, rot, st],
        out_specs=[pl.BlockSpec((DEC_BB, RET_W), lambda i: (i, 0)), st],
        out_shape=[jax.ShapeDtypeStruct((bsz, RET_W), F32), jax.ShapeDtypeStruct(s0.shape, F32)],
        compiler_params=_params(("parallel",)),
        name="ret_dec",
    )(u, u, u, u, cos, sin, s0)


def _ssd_dec_kernel(xn_ref, z_ref, dt_ref, cst_ref, cw_ref, cb_ref, dtb_ref, alog_ref, dskip_ref, nw_ref,
                    e_ref, et_ref, s_ref, y_ref, co_ref, so_ref):
    w = SSD_CONV_DIM
    xn = xn_ref[...]
    conv = cb_ref[...] + cw_ref[SSD_CONV - 1:SSD_CONV, :] * xn
    for t in range(SSD_CONV - 1):
        conv = conv + cw_ref[t:t + 1, :] * cst_ref[:, t * w:(t + 1) * w]
    for t in range(SSD_CONV - 2):
        co_ref[:, t * w:(t + 1) * w] = cst_ref[:, (t + 1) * w:(t + 2) * w]
    co_ref[:, (SSD_CONV - 2) * w:(SSD_CONV - 1) * w] = xn
    xact = _silu(conv)
    xs = xact[:, :SSD_INNER]

    dt = _softplus(dt_ref[...] + dtb_ref[...])
    decay = jnp.exp(dt * (-jnp.exp(alog_ref[...])))
    dtx = xs * _expand_heads(dt, e_ref)
    et = et_ref[...]
    d1, d2, d3 = _split3(decay)
    dcol = _dot_nt(et, d1) + _dot_nt(et, d2) + _dot_nt(et, d3)

    eye = (lax.broadcasted_iota(jnp.int32, (LANE, LANE), 0) == lax.broadcasted_iota(jnp.int32, (LANE, LANE), 1))
    hg = SSD_HEADS // SSD_GROUPS
    pairs_per_group = hg * SSD_HEAD_DIM // LANE
    rows = []
    for bb in range(DEC_BB):
        pieces = []
        for t in range(SSD_INNER // LANE):
            g = t // pairs_per_group
            ts = slice(t * LANE, (t + 1) * LANE)
            bo = SSD_INNER + g * SSD_STATE
            co = SSD_INNER + (SSD_GROUPS + g) * SSD_STATE
            bmat = jnp.broadcast_to(xact[bb:bb + 1, bo:bo + SSD_STATE], (LANE, SSD_STATE)).astype(BF16)
            upd = _dot(_row_diag(dtx[bb:bb + 1, ts], eye), bmat)
            s_new = dcol[ts, bb:bb + 1] * s_ref[bb, ts, :] + upd
            so_ref[bb, ts, :] = s_new
            c8 = jnp.broadcast_to(xact[bb:bb + 1, co:co + SSD_STATE], (8, SSD_STATE)).astype(BF16)
            pieces.append(_dot_nt(c8, s_new.astype(BF16))[0:1, :])
        rows.append(jnp.concatenate(pieces, axis=1))
    y = jnp.concatenate(rows, axis=0) + dskip_ref[...] * xs
    y_ref[...] = _group_rms_gate(y, z_ref[...], nw_ref[...])


def _ssd_dec(u, lp, conv_state, s0):
    bsz = u.shape[0]
    _, e = _ssd_consts()
    et = e.T
    cw = (SSD_CONV - 1) * SSD_CONV_DIM
    const = lambda shape: pl.BlockSpec(shape, lambda i: (0, 0))
    st = pl.BlockSpec((DEC_BB, SSD_INNER, SSD_STATE), lambda i: (i, 0, 0))
    return pl.pallas_call(
        _ssd_dec_kernel,
        grid=(bsz // DEC_BB,),
        in_specs=[pl.BlockSpec((DEC_BB, SSD_CONV_DIM), lambda i: (i, OFF_XBC // SSD_CONV_DIM)),
                  pl.BlockSpec((DEC_BB, SSD_INNER), lambda i: (i, OFF_Z // SSD_INNER)),
                  pl.BlockSpec((DEC_BB, LANE), lambda i: (i, OFF_DT // LANE)),
                  pl.BlockSpec((DEC_BB, cw), lambda i: (i, 0)),
                  const((SSD_CONV, SSD_CONV_DIM)), const((1, SSD_CONV_DIM)),
                  const((1, LANE)), const((1, LANE)), const((1, SSD_INNER)), const((1, SSD_INNER)),
                  const((LANE, SSD_INNER)), const((SSD_INNER, LANE)), st],
        out_specs=[pl.BlockSpec((DEC_BB, SSD_INNER), lambda i: (i, 0)),
                   pl.BlockSpec((DEC_BB, cw), lambda i: (i, 0)), st],
        out_shape=[jax.ShapeDtypeStruct((bsz, SSD_INNER), F32),
                   jax.ShapeDtypeStruct((bsz, cw), F32),
                   jax.ShapeDtypeStruct(s0.shape, F32)],
        compiler_params=_params(("parallel",)),
        name="ssd_dec",
    )(u, u, u, conv_state, lp["conv_w"], lp["conv_b"], lp["dt_bias"], lp["a_log"], lp["d_skip"],
      lp["norm_w"], e, et, s0)


def _per_head_lanes(x, swap):
    lane = lax.broadcasted_iota(jnp.int32, x.shape, 1)
    first = lane < SWA_HEAD_DIM
    a = jnp.where(first, x, swap)
    b = jnp.where(first, swap, x)
    return jnp.concatenate([a, a, b, b], axis=1)


def _swa_dec_kernel(rel_ref, sink_ref, q_ref, kn_ref, vn_ref, idx_ref, kc_ref, vc_ref, o_ref, ko_ref, vo_ref):
    bias = _build_bias(rel_ref, idx_ref[...], (8, 2 * LANE))
    rowi = lax.broadcasted_iota(jnp.int32, (SWA_Q_HEADS, 1), 0)
    bias_c = jnp.zeros((SWA_Q_HEADS, WINDOW), F32)
    bias_n = jnp.zeros((SWA_Q_HEADS, 1), F32)
    sink = jnp.zeros((SWA_Q_HEADS, 1), F32)
    for h in range(SWA_Q_HEADS):
        bias_c = jnp.where(rowi == h, bias[h][0:1, :WINDOW], bias_c)
        bias_n = jnp.where(rowi == h, bias[h][0:1, WINDOW:WINDOW + 1], bias_n)
        sink = jnp.where(rowi == h, sink_ref[h], sink)
    own = (lax.broadcasted_iota(jnp.int32, (SWA_Q_HEADS, SWA_QW), 1) // SWA_HEAD_DIM
           == lax.broadcasted_iota(jnp.int32, (SWA_Q_HEADS, SWA_QW), 0))
    scale = SWA_HEAD_DIM ** -0.5
    q = q_ref[...]
    kn = kn_ref[...]
    vn = vn_ref[...]
    knx = _per_head_lanes(kn, pltpu.roll(kn, SWA_HEAD_DIM, 1))
    vnx = _per_head_lanes(vn, pltpu.roll(vn, SWA_HEAD_DIM, 1))
    rows = []
    for bb in range(DEC_BB):
        kc = kc_ref[bb]
        vc = vc_ref[bb]
        ko_ref[bb, 0:WINDOW - 1, :] = kc[1:WINDOW, :]
        ko_ref[bb, WINDOW - 1:WINDOW, :] = kn[bb:bb + 1, :]
        vo_ref[bb, 0:WINDOW - 1, :] = vc[1:WINDOW, :]
        vo_ref[bb, WINDOW - 1:WINDOW, :] = vn[bb:bb + 1, :]
        qx = jnp.where(own, jnp.broadcast_to(q[bb:bb + 1, :], (SWA_Q_HEADS, SWA_QW)), 0.0)
        kx = _per_head_lanes(kc, pltpu.roll(kc, SWA_HEAD_DIM, 1)).astype(BF16)
        vx = _per_head_lanes(vc, pltpu.roll(vc, SWA_HEAD_DIM, 1)).astype(BF16)
        qxb = qx.astype(BF16)
        s_c = _dot_nt(qxb, kx) * scale + bias_c
        knb = knx[bb:bb + 1, :].astype(BF16).astype(F32)
        s_n = jnp.sum(qxb.astype(F32) * knb, axis=-1, keepdims=True) * scale + bias_n
        m = jnp.maximum(jnp.maximum(jnp.max(s_c, axis=-1, keepdims=True), s_n), sink)
        e_c = jnp.exp(s_c - m)
        e_n = jnp.exp(s_n - m)
        den = jnp.sum(e_c, axis=-1, keepdims=True) + e_n + jnp.exp(sink - m)
        pv = _dot((e_c / den).astype(BF16), vx) + (e_n / den) * vnx[bb:bb + 1, :]
        rows.append(jnp.sum(jnp.where(own, pv, 0.0), axis=0, keepdims=True))
    o_ref[...] = jnp.concatenate(rows, axis=0)


def _swa_dec(u, rel_bias, sinks, kcache, vcache):
    bsz = u.shape[0]
    dist = np.zeros((8, 2 * LANE), np.int64)
    dist[:, :WINDOW + 1] = WINDOW - np.arange(WINDOW + 1)[None, :]
    idx = jnp.asarray(_t5_bucket_np(dist))
    smem = pl.BlockSpec(memory_space=pltpu.SMEM)
    cache = pl.BlockSpec((DEC_BB, WINDOW, SWA_KW), lambda i: (i, 0, 0))
    return pl.pallas_call(
        _swa_dec_kernel,
        grid=(bsz // DEC_BB,),
        in_specs=[smem, smem,
                  pl.BlockSpec((DEC_BB, SWA_QW), lambda i: (i, OFF_QC // SWA_QW)),
                  pl.BlockSpec((DEC_BB, SWA_KW), lambda i: (i, OFF_KC // SWA_KW)),
                  pl.BlockSpec((DEC_BB, SWA_KW), lambda i: (i, OFF_VC // SWA_KW)),
                  pl.BlockSpec((8, 2 * LANE), lambda i: (0, 0)), cache, cache],
        out_specs=[pl.BlockSpec((DEC_BB, SWA_QW), lambda i: (i, 0)), cache, cache],
        out_shape=[jax.ShapeDtypeStruct((bsz, SWA_QW), F32),
                   jax.ShapeDtypeStruct(kcache.shape, F32), jax.ShapeDtypeStruct(vcache.shape, F32)],
        compiler_params=_params(("parallel",)),
        name="swa_dec",
    )(rel_bias, sinks, u, u, u, idx, kcache, vcache)


def _rope_tables(pos):
    half = RET_DK // 2
    inv = ROPE_BASE ** (-jnp.arange(half, dtype=F32) / half)
    ang = pos[:, None] * inv[None, :]
    cos = jnp.cos(ang)
    sin = jnp.sin(ang)
    return jnp.concatenate([cos, cos], -1), jnp.concatenate([-sin, sin], -1)


def _reorder_w_in(w):
    sizes = (RET_W, RET_W, RET_W, RET_W, SSD_INNER, SSD_CONV_DIM, SSD_HEADS, SWA_QW, SWA_KW, SWA_KW, 3 * D_MODEL)
    offs = np.concatenate([[0], np.cumsum(sizes)])
    sec = [w[:, offs[i]:offs[i + 1]] for i in range(len(sizes))]
    q_r, k_r, v_r, g_r, z, xbc, dt, q_c, k_c, v_c, gates = sec
    pad = jnp.zeros((w.shape[0], N_PROJ - OFF_DT - SSD_HEADS), w.dtype)
    return jnp.concatenate([gates, q_r, k_r, v_r, g_r, z, xbc, q_c, k_c, v_c, dt, pad], axis=1).astype(BF16)


def _layer_params(l, w_in, conv_w, conv_b, dt_bias, a_log, d_skip, ssd_norm_w, sinks,
                  w_br_ret, w_br_ssd, w_br_swa, w_out, ln1_g, ln1_b, ln2_g, ln2_b,
                  w_ffn_gate, w_ffn_up, w_ffn_down):
    row = lambda v: v[l].astype(F32)[None, :]
    return {
        "w_in": _reorder_w_in(w_in[l]),
        "conv_w": conv_w[l], "conv_b": row(conv_b),
        "dt_bias": _pad_lanes(dt_bias[l]), "a_log": _pad_lanes(a_log[l]),
        "d_skip": jnp.repeat(d_skip[l].astype(F32), SSD_HEAD_DIM)[None, :],
        "norm_w": row(ssd_norm_w), "sinks": sinks[l],
        "w_br_ret": w_br_ret[l].astype(BF16), "w_br_ssd": w_br_ssd[l].astype(BF16),
        "w_br_swa": w_br_swa[l].astype(BF16), "w_out": w_out[l].astype(BF16),
        "ln1_g": row(ln1_g), "ln1_b": row(ln1_b), "ln2_g": row(ln2_g), "ln2_b": row(ln2_b),
        "w_ffn_gate": w_ffn_gate[l].astype(BF16), "w_ffn_up": w_ffn_up[l].astype(BF16),
        "w_ffn_down": w_ffn_down[l].astype(BF16),
    }


def _prompt_layer(x, xb, lp, rel_bias, cos, sin, bsz, seq):
    u = _matmul(xb, lp["w_in"], min(1024, xb.shape[0]), PROJ_TN)
    o_r, s_ret = _ret_prompt(u, cos, sin, bsz, seq)
    y, s_ssm = _ssd_prompt(u, lp, bsz, seq)
    o_c = _swa_prompt(u, rel_bias, lp["sinks"], bsz, seq)
    x1, x1b = _merge(o_r, y, o_c, u, x, lp, 512)
    x2, x2b = _ffn(x1b, x1, lp, 512)
    u3 = u.reshape(bsz, seq, N_PROJ)
    conv = u3[:, seq - (SSD_CONV - 1):, OFF_XBC:OFF_XBC + SSD_CONV_DIM]
    k_new = u3[:, seq - WINDOW:, OFF_KC:OFF_KC + SWA_KW].reshape(bsz, WINDOW, SWA_KV_HEADS, SWA_HEAD_DIM)
    v_new = u3[:, seq - WINDOW:, OFF_VC:OFF_VC + SWA_KW].reshape(bsz, WINDOW, SWA_KV_HEADS, SWA_HEAD_DIM)
    s_ssm = s_ssm.reshape(bsz, SSD_HEADS, SSD_HEAD_DIM, SSD_STATE)
    return x2, x2b, (s_ret, s_ssm, conv, k_new, v_new)


def _sample_layer(x, xb, lp, rel_bias, cos, sin, states):
    s_ret, s_ssm, s_conv, kbuf, vbuf = states
    bsz = x.shape[0]
    u = _matmul(xb, lp["w_in"], bsz, PROJ_TN)
    o_r, s_ret_new = _ret_dec(u, cos, sin, s_ret)
    y, conv_new, s_ssm_new = _ssd_dec(u, lp, s_conv.reshape(bsz, (SSD_CONV - 1) * SSD_CONV_DIM),
                                      s_ssm.reshape(bsz, SSD_INNER, SSD_STATE))
    o_c, k_new, v_new = _swa_dec(u, rel_bias, lp["sinks"], kbuf.reshape(bsz, WINDOW, SWA_KW),
                                 vbuf.reshape(bsz, WINDOW, SWA_KW))
    x1, x1b = _merge(o_r.astype(BF16), y.astype(BF16), o_c.astype(BF16), u, x, lp, bsz)
    x2, x2b = _ffn(x1b, x1, lp, bsz)
    return x2, x2b, (s_ret_new, s_ssm_new.reshape(s_ssm.shape), conv_new.reshape(s_conv.shape),
                     k_new.reshape(kbuf.shape), v_new.reshape(vbuf.shape))


def kernel(x_prompt, x_sample, state_ret, state_ssm, state_conv, cache_swa_k, cache_swa_v, w_in, conv_w, conv_b, dt_bias, a_log, d_skip, ssd_norm_w, sinks, rel_bias, w_br_ret, w_br_ssd, w_br_swa, w_out, ln1_g, ln1_b, ln2_g, ln2_b, w_ffn_gate, w_ffn_up, w_ffn_down):
    weights = (w_in, conv_w, conv_b, dt_bias, a_log, d_skip, ssd_norm_w, sinks,
               w_br_ret, w_br_ssd, w_br_swa, w_out, ln1_g, ln1_b, ln2_g, ln2_b,
               w_ffn_gate, w_ffn_up, w_ffn_down)
    bsz, seq, _ = x_prompt.shape
    dbsz, dseq, _ = x_sample.shape
    assert seq % CHUNK == 0 and dseq == 1 and dbsz % DEC_BB == 0
    cos_p, sin_p = _rope_tables(jnp.arange(seq, dtype=F32))
    cos_s, sin_s = _rope_tables(PAST_LEN + jnp.arange(dseq, dtype=F32))
    rel = rel_bias.astype(F32)

    xp = x_prompt.reshape(bsz * seq, D_MODEL).astype(F32)
    xs = x_sample.reshape(dbsz * dseq, D_MODEL).astype(F32)
    xpb, xsb = xp.astype(BF16), xs.astype(BF16)
    new_p = [[] for _ in range(5)]
    new_s = [[] for _ in range(5)]
    for l in range(DEPTH):
        lp = _layer_params(l, *weights)
        xp, xpb, st_p = _prompt_layer(xp, xpb, lp, rel, cos_p, sin_p, bsz, seq)
        xs, xsb, st_s = _sample_layer(xs, xsb, lp, rel, cos_s, sin_s,
                                      (state_ret[l], state_ssm[l], state_conv[l], cache_swa_k[l], cache_swa_v[l]))
        for j in range(5):
            new_p[j].append(st_p[j])
            new_s[j].append(st_s[j])
    st_p = [jnp.stack(s) for s in new_p]
    st_s = [jnp.stack(s) for s in new_s]
    return (xp.reshape(bsz, seq, D_MODEL), xs.reshape(dbsz, dseq, D_MODEL), *st_p, *st_s)
```

```python
import functools
import math

import jax
import jax.numpy as jnp
import numpy as np
from jax import lax
from jax.experimental import pallas as pl
from jax.experimental.pallas import tpu as pltpu

F32 = jnp.float32
BF16 = jnp.bfloat16

D_MODEL = 1024
DEPTH = 4
PAST_LEN = 8192
RET_HEADS = 4
RET_DK = 128
RET_DV = 128
ROPE_BASE = 10000.0
SSD_HEADS = 16
SSD_HEAD_DIM = 64
SSD_GROUPS = 2
SSD_STATE = 128
SSD_CONV = 4
SSD_INNER = SSD_HEADS * SSD_HEAD_DIM
SSD_CONV_DIM = SSD_INNER + 2 * SSD_GROUPS * SSD_STATE
SWA_Q_HEADS = 8
SWA_KV_HEADS = 2
SWA_HEAD_DIM = 64
WINDOW = 128
N_BUCKETS = 32
BUCKET_MAX_DIST = 128
D_FF = ((8 * D_MODEL // 3 + 255) // 256) * 256
ALPHA = (2 * DEPTH) ** 0.25
EPS = 1e-5
CHUNK = 128

RET_W = RET_HEADS * RET_DK
SWA_QW = SWA_Q_HEADS * SWA_HEAD_DIM
SWA_KW = SWA_KV_HEADS * SWA_HEAD_DIM
CONV_ST_W = (SSD_CONV - 1) * SSD_CONV_DIM

OFF_GATES = 0
OFF_QR = 3 * D_MODEL
OFF_KR = OFF_QR + RET_W
OFF_VR = OFF_KR + RET_W
OFF_GR = OFF_VR + RET_W
OFF_Z = OFF_GR + RET_W
OFF_XBC = OFF_Z + SSD_INNER
OFF_QC = OFF_XBC + SSD_CONV_DIM
OFF_KC = OFF_QC + SWA_QW
OFF_VC = OFF_KC + SWA_KW
N_U = OFF_VC + SWA_KW
LANE = 128
N_PROJ = N_U + LANE
PROJ_CHUNK = 1024

VMEM_LIMIT = 48 * 1024 * 1024
NEG_BIG = -1e30


def _sigmoid(x):
    return 1.0 / (1.0 + jnp.exp(-x))


def _silu(x):
    return x * _sigmoid(x)


def _softplus(x):
    return jnp.maximum(x, 0.0) + jnp.log1p(jnp.exp(-jnp.abs(x)))


def _dot(a, b):
    return jnp.dot(a, b, preferred_element_type=F32)


def _dot_nt(a, b):
    return lax.dot_general(a, b, (((1,), (1,)), ((), ())), preferred_element_type=F32)


def _dot_tn(a, b):
    return lax.dot_general(a, b, (((0,), (0,)), ((), ())), preferred_element_type=F32)


def _split2(x):
    hi = x.astype(BF16)
    lo = (x - hi.astype(F32)).astype(BF16)
    return hi, lo


def _split3(x):
    hi = x.astype(BF16)
    r = x - hi.astype(F32)
    mid = r.astype(BF16)
    lo = (r - mid.astype(F32)).astype(BF16)
    return hi, mid, lo


def _params(sem):
    return pltpu.CompilerParams(dimension_semantics=sem, vmem_limit_bytes=VMEM_LIMIT)


def _layer_spec(l, shape, single_buffer=False):
    nd = len(shape)
    mode = pl.Buffered(1) if single_buffer else None
    return pl.BlockSpec((None,) + tuple(shape), lambda *_: (l,) + (0,) * nd, pipeline_mode=mode)


def _const_spec(shape):
    nd = len(shape)
    return pl.BlockSpec(tuple(shape), lambda *_: (0,) * nd)


_ANY = pl.BlockSpec(memory_space=pl.ANY)


def _in_proj_kernel(x_ref, w_ref, u_ref, dt_ref):
    x = x_ref[...]
    for a in range(0, N_U, PROJ_CHUNK):
        b = min(a + PROJ_CHUNK, N_U)
        u_ref[:, a:b] = _dot(x, w_ref[:, a:b]).astype(u_ref.dtype)
    dt_ref[...] = _dot(x, w_ref[:, N_U:N_PROJ])


def _in_proj(xb, w_all, l, tm, u_dtype):
    m = xb.shape[0]
    return pl.pallas_call(
        _in_proj_kernel,
        grid=(m // tm,),
        in_specs=[pl.BlockSpec((tm, D_MODEL), lambda i: (i, 0)),
                  _layer_spec(l, (D_MODEL, N_PROJ), single_buffer=True)],
        out_specs=[pl.BlockSpec((tm, N_U), lambda i: (i, 0)), pl.BlockSpec((tm, LANE), lambda i: (i, 0))],
        out_shape=[jax.ShapeDtypeStruct((m, N_U), u_dtype), jax.ShapeDtypeStruct((m, LANE), F32)],
        compiler_params=_params(("parallel",)),
        name="in_proj",
    )(xb, w_all)


def _rotary(x, cos, sin):
    return x * cos + pltpu.roll(x, RET_DK // 2, 1) * sin


def _head_norm_gate(o, g):
    mu = jnp.mean(o, axis=-1, keepdims=True)
    d = o - mu
    var = jnp.mean(d * d, axis=-1, keepdims=True)
    return d * lax.rsqrt(var + EPS) * _silu(g)


def _ret_prompt_kernel(q_ref, k_ref, v_ref, g_ref, cos_ref, sin_ref, dmat_ref, kdec_ref, qdec_ref,
                       o_ref, sfin_ref, s_scr, *, cdec):
    c = pl.program_id(1)

    @pl.when(c == 0)
    def _():
        s_scr[...] = jnp.zeros_like(s_scr)

    cos = cos_ref[...]
    sin = sin_ref[...]
    for h in range(RET_HEADS):
        sl = slice(h * RET_DK, (h + 1) * RET_DK)
        qr = _rotary(q_ref[:, sl].astype(F32), cos, sin)
        kr = _rotary(k_ref[:, sl].astype(F32), cos, sin) * (RET_DK ** -0.5)
        qb = qr.astype(BF16)
        vb = v_ref[:, sl]
        sc = _dot_nt(qb, kr.astype(BF16)) * dmat_ref[h]
        s_prev = s_scr[h]
        o = _dot(sc.astype(BF16), vb) + _dot(qb, s_prev.astype(BF16)) * qdec_ref[:, sl]
        kd = (kr * kdec_ref[:, sl]).astype(BF16)
        s_scr[h] = cdec[h] * s_prev + _dot_tn(kd, vb)
        o_ref[:, sl] = _head_norm_gate(o, g_ref[:, sl].astype(F32)).astype(o_ref.dtype)

    @pl.when(c == pl.num_programs(1) - 1)
    def _():
        sfin_ref[0] = s_scr[...]


def _ret_consts(c):
    lg = [math.log(1.0 - 2.0 ** (-5.0 - h)) for h in range(RET_HEADS)]
    i = np.arange(c, dtype=np.float64)
    rel = i[:, None] - i[None, :]
    dmat = np.stack([np.where(rel >= 0, np.exp(l * np.maximum(rel, 0.0)), 0.0) for l in lg])
    kdec = np.concatenate([np.repeat(np.exp(l * (c - 1 - i))[:, None], RET_DK, 1) for l in lg], 1)
    qdec = np.concatenate([np.repeat(np.exp(l * (i + 1.0))[:, None], RET_DK, 1) for l in lg], 1)
    cdec = tuple(math.exp(l * c) for l in lg)
    return (jnp.asarray(dmat, F32), jnp.asarray(kdec, F32), jnp.asarray(qdec, F32), cdec)


def _ret_prompt(u, cos, sin, bsz, seq):
    nc = seq // CHUNK
    dmat, kdec, qdec, cdec = _ret_consts(CHUNK)
    row = lambda b, c: b * nc + c
    usec = lambda off: pl.BlockSpec((CHUNK, RET_W), lambda b, c: (row(b, c), off // RET_W))
    return pl.pallas_call(
        functools.partial(_ret_prompt_kernel, cdec=cdec),
        grid=(bsz, nc),
        in_specs=[usec(OFF_QR), usec(OFF_KR), usec(OFF_VR), usec(OFF_GR),
                  pl.BlockSpec((CHUNK, RET_DK), lambda b, c: (c, 0)),
                  pl.BlockSpec((CHUNK, RET_DK), lambda b, c: (c, 0)),
                  _const_spec((RET_HEADS, CHUNK, CHUNK)), _const_spec((CHUNK, RET_W)),
                  _const_spec((CHUNK, RET_W))],
        out_specs=[pl.BlockSpec((CHUNK, RET_W), lambda b, c: (row(b, c), 0)),
                   pl.BlockSpec((1, RET_HEADS, RET_DK, RET_DV), lambda b, c: (b, 0, 0, 0))],
        out_shape=[jax.ShapeDtypeStruct((bsz * seq, RET_W), BF16),
                   jax.ShapeDtypeStruct((bsz, RET_HEADS, RET_DK, RET_DV), F32)],
        scratch_shapes=[pltpu.VMEM((RET_HEADS, RET_DK, RET_DV), F32)],
        compiler_params=_params(("arbitrary", "arbitrary")),
        name="ret_prompt",
    )(u, u, u, u, cos, sin, dmat, kdec, qdec)


def _expand_heads(v, e_ref):
    hi, lo = _split2(v)
    e = e_ref[...]
    return _dot(hi, e) + _dot(lo, e)


def _group_rms_gate(y, z, w):
    y = y * _silu(z)
    half = SSD_INNER // SSD_GROUPS
    outs = []
    for g in range(SSD_GROUPS):
        yg = y[:, g * half:(g + 1) * half]
        ms = jnp.mean(yg * yg, axis=-1, keepdims=True)
        outs.append(yg * lax.rsqrt(ms + EPS))
    return jnp.concatenate(outs, axis=-1) * w


def _ssd_prompt_kernel(xbc_ref, z_ref, dt_ref, cw_ref, cb_ref, dtb_ref, alog_ref, dskip_ref, nw_ref,
                       tril_ref, e_ref, y_ref, sfin_ref, ext_scr, st_scr, yi_scr):
    c = pl.program_id(1)
    pad = 8

    @pl.when(c == 0)
    def _():
        ext_scr[0:pad, :] = jnp.zeros((pad, SSD_CONV_DIM), F32)
        st_scr[...] = jnp.zeros_like(st_scr)

    @pl.when(c > 0)
    def _():
        ext_scr[0:pad, :] = ext_scr[CHUNK:CHUNK + pad, :]

    ext_scr[pad:pad + CHUNK, :] = xbc_ref[...].astype(F32)
    conv = cb_ref[...] + sum(
        cw_ref[t:t + 1, :] * ext_scr[pad - (SSD_CONV - 1) + t:pad - (SSD_CONV - 1) + t + CHUNK, :]
        for t in range(SSD_CONV))
    xact = _silu(conv)
    xs = xact[:, :SSD_INNER]
    xs_b = xs.astype(BF16)

    dt = _softplus(dt_ref[...] + dtb_ref[...])
    a = -jnp.exp(alog_ref[...])
    da = dt * a
    tril = tril_ref[...]
    d1, d2, d3 = _split3(da)
    cs = _dot(tril, d1) + _dot(tril, d2) + _dot(tril, d3)
    cs_last = cs[CHUNK - 1:CHUNK, :]
    cs_t = cs.T
    dt_t = dt.T
    exp_cs = _expand_heads(jnp.exp(cs), e_ref)
    xw = (xs * _expand_heads(jnp.exp(cs_last - cs) * dt, e_ref)).astype(BF16)

    ri = lax.broadcasted_iota(jnp.int32, (CHUNK, CHUNK), 0)
    ci = lax.broadcasted_iota(jnp.int32, (CHUNK, CHUNK), 1)
    causal = ri >= ci
    hg = SSD_HEADS // SSD_GROUPS
    gw = hg * SSD_HEAD_DIM
    for g in range(SSD_GROUPS):
        bg = xact[:, SSD_INNER + g * SSD_STATE:SSD_INNER + (g + 1) * SSD_STATE].astype(BF16)
        cg = xact[:, SSD_INNER + (SSD_GROUPS + g) * SSD_STATE:
                  SSD_INNER + (SSD_GROUPS + g + 1) * SSD_STATE].astype(BF16)
        cb = _dot_nt(cg, bg)
        for hh in range(hg):
            h = g * hg + hh
            seg = cs[:, h:h + 1] - cs_t[h:h + 1, :]
            lmat = jnp.exp(jnp.where(causal, seg, NEG_BIG))
            w = (cb * lmat * dt_t[h:h + 1, :]).astype(BF16)
            hs = slice(h * SSD_HEAD_DIM, (h + 1) * SSD_HEAD_DIM)
            yi_scr[:, hs] = _dot(w, xs_b[:, hs])
        gs = slice(g * gw, (g + 1) * gw)
        st_prev = st_scr[:, gs]
        yi_scr[:, gs] = yi_scr[:, gs] + _dot(cg, st_prev.astype(BF16)) * exp_cs[:, gs]
        st_scr[:, gs] = st_prev * exp_cs[CHUNK - 1:CHUNK, gs] + _dot_tn(bg, xw[:, gs])

    y = yi_scr[...] + dskip_ref[...] * xs
    y_ref[...] = _group_rms_gate(y, z_ref[...].astype(F32), nw_ref[...]).astype(y_ref.dtype)

    @pl.when(c == pl.num_programs(1) - 1)
    def _():
        for t in range(SSD_INNER // LANE):
            sfin_ref[0, t * LANE:(t + 1) * LANE, :] = st_scr[:, t * LANE:(t + 1) * LANE].T


def _ssd_consts():
    tril = np.tril(np.ones((CHUNK, CHUNK), np.float32))
    e = np.zeros((LANE, SSD_INNER), np.float32)
    for h in range(SSD_HEADS):
        e[h, h * SSD_HEAD_DIM:(h + 1) * SSD_HEAD_DIM] = 1.0
    return jnp.asarray(tril, BF16), jnp.asarray(e, BF16)


def _ssd_param_specs(l):
    return [_layer_spec(l, (SSD_CONV, SSD_CONV_DIM)), _layer_spec(l, (1, SSD_CONV_DIM)),
            _layer_spec(l, (1, LANE)), _layer_spec(l, (1, LANE)),
            _layer_spec(l, (1, SSD_INNER)), _layer_spec(l, (1, SSD_INNER))]


def _ssd_param_args(p):
    return (p["conv_w"], p["conv_b"], p["dt_bias"], p["a_log"], p["d_skip"], p["norm_w"])


def _ssd_prompt(u, dt, p, l, bsz, seq):
    nc = seq // CHUNK
    tril, e = _ssd_consts()
    row = lambda b, c: b * nc + c
    return pl.pallas_call(
        _ssd_prompt_kernel,
        grid=(bsz, nc),
        in_specs=[pl.BlockSpec((CHUNK, SSD_CONV_DIM), lambda b, c: (row(b, c), OFF_XBC // SSD_CONV_DIM)),
                  pl.BlockSpec((CHUNK, SSD_INNER), lambda b, c: (row(b, c), OFF_Z // SSD_INNER)),
                  pl.BlockSpec((CHUNK, LANE), lambda b, c: (row(b, c), 0)),
                  *_ssd_param_specs(l),
                  _const_spec((CHUNK, CHUNK)), _const_spec((LANE, SSD_INNER))],
        out_specs=[pl.BlockSpec((CHUNK, SSD_INNER), lambda b, c: (row(b, c), 0)),
                   pl.BlockSpec((1, SSD_INNER, SSD_STATE), lambda b, c: (b, 0, 0))],
        out_shape=[jax.ShapeDtypeStruct((bsz * seq, SSD_INNER), BF16),
                   jax.ShapeDtypeStruct((bsz, SSD_INNER, SSD_STATE), F32)],
        scratch_shapes=[pltpu.VMEM((CHUNK + 8, SSD_CONV_DIM), F32),
                        pltpu.VMEM((SSD_STATE, SSD_INNER), F32),
                        pltpu.VMEM((CHUNK, SSD_INNER), F32)],
        compiler_params=_params(("arbitrary", "arbitrary")),
        name="ssd_prompt",
    )(u, u, dt, *_ssd_param_args(p), tril, e)


def _t5_bucket_np(dist):
    max_exact = N_BUCKETS // 2
    df = np.maximum(dist, 1).astype(np.float32)
    large = max_exact + (np.log(df / np.float32(max_exact)) / np.float32(math.log(BUCKET_MAX_DIST / max_exact))
                         * np.float32(N_BUCKETS - max_exact)).astype(np.int32)
    large = np.minimum(large, N_BUCKETS - 1)
    return np.where(dist < max_exact, dist, large).astype(np.int32)


def _build_bias(rel_ref, idx, shape):
    out = []
    for h in range(SWA_Q_HEADS):
        acc = jnp.zeros(shape, F32)
        for b in range(N_BUCKETS):
            acc = jnp.where(idx == b, rel_ref[b, h], acc)
        out.append(acc)
    return out


def _swa_prompt_kernel(rel_ref, sink_ref, q_ref, kp_ref, kc_ref, vp_ref, vc_ref, idx_ref, o_ref, bias_scr,
                       *, layer):
    b = pl.program_id(0)
    n = pl.program_id(1)

    @pl.when(jnp.logical_and(b == 0, n == 0))
    def _():
        for h, bias in enumerate(_build_bias(rel_ref, idx_ref[...], (CHUNK, 2 * CHUNK))):
            bias_scr[h] = bias

    qi = lax.broadcasted_iota(jnp.int32, (CHUNK, 2 * CHUNK), 0)
    kj = lax.broadcasted_iota(jnp.int32, (CHUNK, 2 * CHUNK), 1)
    dist = qi + WINDOW - kj
    band = jnp.logical_and(dist >= 0, dist <= WINDOW)
    key_ok = jnp.logical_or(kj >= WINDOW, n > 0)
    mask = jnp.logical_and(band, key_ok)
    kk = jnp.concatenate([kp_ref[...], kc_ref[...]], axis=0)
    vv = jnp.concatenate([vp_ref[...], vc_ref[...]], axis=0)
    grp = SWA_Q_HEADS // SWA_KV_HEADS
    for h in range(SWA_Q_HEADS):
        kv = h // grp
        ks = slice(kv * SWA_HEAD_DIM, (kv + 1) * SWA_HEAD_DIM)
        qh = q_ref[:, h * SWA_HEAD_DIM:(h + 1) * SWA_HEAD_DIM]
        s = _dot_nt(qh, kk[:, ks]) * (SWA_HEAD_DIM ** -0.5) + bias_scr[h]
        logits = jnp.where(mask, s, NEG_BIG)
        sink = sink_ref[layer, h]
        m = jnp.maximum(jnp.max(logits, axis=-1, keepdims=True), sink)
        e = jnp.exp(logits - m)
        den = jnp.sum(e, axis=-1, keepdims=True) + jnp.exp(sink - m)
        p = (e / den).astype(BF16)
        o_ref[:, h * SWA_HEAD_DIM:(h + 1) * SWA_HEAD_DIM] = _dot(p, vv[:, ks]).astype(o_ref.dtype)


def _swa_prompt(u, rel_bias, sinks, l, bsz, seq):
    nb = seq // CHUNK
    qi = np.arange(CHUNK)[:, None]
    kj = np.arange(2 * CHUNK)[None, :]
    idx = jnp.asarray(_t5_bucket_np(np.maximum(qi + WINDOW - kj, 0)))
    row = lambda b, n: b * nb + n
    prev = lambda b, n: b * nb + jnp.maximum(n - 1, 0)
    smem = pl.BlockSpec(memory_space=pltpu.SMEM)
    return pl.pallas_call(
        functools.partial(_swa_prompt_kernel, layer=l),
        grid=(bsz, nb),
        in_specs=[smem, smem,
                  pl.BlockSpec((CHUNK, SWA_QW), lambda b, n: (row(b, n), OFF_QC // SWA_QW)),
                  pl.BlockSpec((CHUNK, SWA_KW), lambda b, n: (prev(b, n), OFF_KC // SWA_KW)),
                  pl.BlockSpec((CHUNK, SWA_KW), lambda b, n: (row(b, n), OFF_KC // SWA_KW)),
                  pl.BlockSpec((CHUNK, SWA_KW), lambda b, n: (prev(b, n), OFF_VC // SWA_KW)),
                  pl.BlockSpec((CHUNK, SWA_KW), lambda b, n: (row(b, n), OFF_VC // SWA_KW)),
                  _const_spec((CHUNK, 2 * CHUNK))],
        out_specs=pl.BlockSpec((CHUNK, SWA_QW), lambda b, n: (row(b, n), 0)),
        out_shape=jax.ShapeDtypeStruct((bsz * seq, SWA_QW), BF16),
        scratch_shapes=[pltpu.VMEM((SWA_Q_HEADS, CHUNK, 2 * CHUNK), F32)],
        compiler_params=_params(("arbitrary", "arbitrary")),
        name="swa_prompt",
    )(rel_bias, sinks, u, u, u, u, u, idx)


def _layer_norm(r, g, b):
    mu = jnp.mean(r, axis=-1, keepdims=True)
    d = r - mu
    var = jnp.mean(d * d, axis=-1, keepdims=True)
    return d * lax.rsqrt(var + EPS) * g + b


def _merge_kernel(or_ref, y_ref, oc_ref, ga_ref, gb_ref, gc_ref, x_ref, wa_ref, wb_ref, wc_ref, wo_ref,
                  g_ref, b_ref, xo_ref, xob_ref):
    mix = (_sigmoid(ga_ref[...].astype(F32)) * _dot(or_ref[...], wa_ref[...])
           + _sigmoid(gb_ref[...].astype(F32)) * _dot(y_ref[...], wb_ref[...])
           + _sigmoid(gc_ref[...].astype(F32)) * _dot(oc_ref[...], wc_ref[...]))
    r = ALPHA * x_ref[...] + _dot(mix.astype(BF16), wo_ref[...])
    xn = _layer_norm(r, g_ref[...], b_ref[...])
    xo_ref[...] = xn
    xob_ref[...] = xn.astype(BF16)


def _merge(o_r, y, o_c, u, x, p, l, tm):
    m = x.shape[0]
    rows = lambda w: pl.BlockSpec((tm, w), lambda i: (i, 0))
    gate = lambda k: pl.BlockSpec((tm, D_MODEL), lambda i: (i, OFF_GATES // D_MODEL + k))
    wspec = lambda k: _layer_spec(l, (k, D_MODEL), single_buffer=True)
    return pl.pallas_call(
        _merge_kernel,
        grid=(m // tm,),
        in_specs=[rows(RET_W), rows(SSD_INNER), rows(SWA_QW), gate(0), gate(1), gate(2), rows(D_MODEL),
                  wspec(RET_W), wspec(SSD_INNER), wspec(SWA_QW), wspec(D_MODEL),
                  _layer_spec(l, (1, D_MODEL)), _layer_spec(l, (1, D_MODEL))],
        out_specs=[rows(D_MODEL), rows(D_MODEL)],
        out_shape=[jax.ShapeDtypeStruct((m, D_MODEL), F32), jax.ShapeDtypeStruct((m, D_MODEL), BF16)],
        compiler_params=_params(("parallel",)),
        name="merge",
    )(o_r, y, o_c, u, u, u, x, p["w_br_ret"], p["w_br_ssd"], p["w_br_swa"], p["w_out"],
      p["ln1_g"], p["ln1_b"])


def _ffn_kernel(xb_ref, x_ref, wg_ref, wu_ref, wd_ref, g_ref, b_ref, xo_ref, xob_ref):
    xb = xb_ref[...]
    hmid = _silu(_dot(xb, wg_ref[...])) * _dot(xb, wu_ref[...])
    r = ALPHA * x_ref[...] + _dot(hmid.astype(BF16), wd_ref[...])
    xn = _layer_norm(r, g_ref[...], b_ref[...])
    xo_ref[...] = xn
    xob_ref[...] = xn.astype(BF16)


def _ffn(xb, x, p, l, tm):
    m = x.shape[0]
    rows = pl.BlockSpec((tm, D_MODEL), lambda i: (i, 0))
    return pl.pallas_call(
        _ffn_kernel,
        grid=(m // tm,),
        in_specs=[rows, rows,
                  _layer_spec(l, (D_MODEL, D_FF), single_buffer=True),
                  _layer_spec(l, (D_MODEL, D_FF), single_buffer=True),
                  _layer_spec(l, (D_FF, D_MODEL), single_buffer=True),
                  _layer_spec(l, (1, D_MODEL)), _layer_spec(l, (1, D_MODEL))],
        out_specs=[rows, rows],
        out_shape=[jax.ShapeDtypeStruct((m, D_MODEL), F32), jax.ShapeDtypeStruct((m, D_MODEL), BF16)],
        compiler_params=_params(("parallel",)),
        name="ffn",
    )(xb, x, p["w_ffn_gate"], p["w_ffn_up"], p["w_ffn_down"], p["ln2_g"], p["ln2_b"])


DEC_BB = 8


def _state_spec(l, shape):
    nd = len(shape)
    return pl.BlockSpec((None, DEC_BB) + tuple(shape), lambda i: (l, i) + (0,) * nd)


def _alias_args(l, n_in, prev):
    if l == 0:
        return [], [], {}
    return [_ANY] * len(prev), list(prev), {n_in + k: 1 + k for k in range(len(prev))}


def _row_diag(row, eye):
    n = row.shape[1]
    return jnp.where(eye, jnp.broadcast_to(row, (n, n)), 0.0).astype(BF16)


def _ret_dec_kernel(q_ref, k_ref, v_ref, g_ref, cos_ref, sin_ref, s_ref, *rest, gamma):
    o_ref, so_ref = rest[-2:]
    eye = (lax.broadcasted_iota(jnp.int32, (RET_DK, RET_DK), 0)
           == lax.broadcasted_iota(jnp.int32, (RET_DK, RET_DK), 1))
    cos = cos_ref[...]
    sin = sin_ref[...]
    for h in range(RET_HEADS):
        sl = slice(h * RET_DK, (h + 1) * RET_DK)
        qr = _rotary(q_ref[:, sl], cos, sin)
        kr = _rotary(k_ref[:, sl], cos, sin) * (RET_DK ** -0.5)
        v = v_ref[:, sl]
        rows = []
        for bb in range(DEC_BB):
            kv = _dot(_row_diag(kr[bb:bb + 1, :], eye),
                      jnp.broadcast_to(v[bb:bb + 1, :], (RET_DK, RET_DV)).astype(BF16))
            s_new = gamma[h] * s_ref[bb, h] + kv
            so_ref[bb, h] = s_new
            q8 = jnp.broadcast_to(qr[bb:bb + 1, :], (8, RET_DK)).astype(BF16)
            rows.append(_dot(q8, s_new.astype(BF16))[0:1, :])
        o = jnp.concatenate(rows, axis=0)
        o_ref[:, sl] = _head_norm_gate(o, g_ref[:, sl])


def _ret_dec(u, cos, sin, s_all, l, prev):
    bsz = u.shape[0]
    gamma = tuple(1.0 - 2.0 ** (-5.0 - h) for h in range(RET_HEADS))
    usec = lambda off: pl.BlockSpec((DEC_BB, RET_W), lambda i: (i, off // RET_W))
    st = _state_spec(l, (RET_HEADS, RET_DK, RET_DV))
    rot = _const_spec((1, RET_DK))
    in_specs = [usec(OFF_QR), usec(OFF_KR), usec(OFF_VR), usec(OFF_GR), rot, rot, st]
    alias_specs, alias_args, aliases = _alias_args(l, len(in_specs), prev)
    return pl.pallas_call(
        functools.partial(_ret_dec_kernel, gamma=gamma),
        grid=(bsz // DEC_BB,),
        in_specs=in_specs + alias_specs,
        out_specs=[pl.BlockSpec((DEC_BB, RET_W), lambda i: (i, 0)), st],
        out_shape=[jax.ShapeDtypeStruct((bsz, RET_W), F32), jax.ShapeDtypeStruct(s_all.shape, F32)],
        input_output_aliases=aliases,
        compiler_params=_params(("parallel",)),
        name="ret_dec",
    )(u, u, u, u, cos, sin, s_all, *alias_args)


def _ssd_dec_kernel(xn_ref, z_ref, dt_ref, cst_ref, cw_ref, cb_ref, dtb_ref, alog_ref, dskip_ref, nw_ref,
                    e_ref, et_ref, s_ref, *rest):
    y_ref, co_ref, so_ref = rest[-3:]
    w = SSD_CONV_DIM
    xn = xn_ref[...]
    conv = cb_ref[...] + cw_ref[SSD_CONV - 1:SSD_CONV, :] * xn
    for t in range(SSD_CONV - 1):
        conv = conv + cw_ref[t:t + 1, :] * cst_ref[:, t * w:(t + 1) * w]
    for t in range(SSD_CONV - 2):
        co_ref[:, t * w:(t + 1) * w] = cst_ref[:, (t + 1) * w:(t + 2) * w]
    co_ref[:, (SSD_CONV - 2) * w:(SSD_CONV - 1) * w] = xn
    xact = _silu(conv)
    xs = xact[:, :SSD_INNER]

    dt = _softplus(dt_ref[...] + dtb_ref[...])
    decay = jnp.exp(dt * (-jnp.exp(alog_ref[...])))
    dtx = xs * _expand_heads(dt, e_ref)
    et = et_ref[...]
    d1, d2, d3 = _split3(decay)
    dcol = _dot_nt(et, d1) + _dot_nt(et, d2) + _dot_nt(et, d3)

    eye = (lax.broadcasted_iota(jnp.int32, (LANE, LANE), 0) == lax.broadcasted_iota(jnp.int32, (LANE, LANE), 1))
    hg = SSD_HEADS // SSD_GROUPS
    pairs_per_group = hg * SSD_HEAD_DIM // LANE
    rows = []
    for bb in range(DEC_BB):
        pieces = []
        for t in range(SSD_INNER // LANE):
            g = t // pairs_per_group
            ts = slice(t * LANE, (t + 1) * LANE)
            bo = SSD_INNER + g * SSD_STATE
            co = SSD_INNER + (SSD_GROUPS + g) * SSD_STATE
            bmat = jnp.broadcast_to(xact[bb:bb + 1, bo:bo + SSD_STATE], (LANE, SSD_STATE)).astype(BF16)
            upd = _dot(_row_diag(dtx[bb:bb + 1, ts], eye), bmat)
            s_new = dcol[ts, bb:bb + 1] * s_ref[bb, ts, :] + upd
            so_ref[bb, ts, :] = s_new
            c8 = jnp.broadcast_to(xact[bb:bb + 1, co:co + SSD_STATE], (8, SSD_STATE)).astype(BF16)
            pieces.append(_dot_nt(c8, s_new.astype(BF16))[0:1, :])
        rows.append(jnp.concatenate(pieces, axis=1))
    y = jnp.concatenate(rows, axis=0) + dskip_ref[...] * xs
    y_ref[...] = _group_rms_gate(y, z_ref[...], nw_ref[...])


def _ssd_dec(u, dt, p, conv_all, s_all, l, prev):
    bsz = u.shape[0]
    _, e = _ssd_consts()
    et = e.T
    st = _state_spec(l, (SSD_INNER, SSD_STATE))
    cst = _state_spec(l, (CONV_ST_W,))
    in_specs = [pl.BlockSpec((DEC_BB, SSD_CONV_DIM), lambda i: (i, OFF_XBC // SSD_CONV_DIM)),
                pl.BlockSpec((DEC_BB, SSD_INNER), lambda i: (i, OFF_Z // SSD_INNER)),
                pl.BlockSpec((DEC_BB, LANE), lambda i: (i, 0)),
                cst, *_ssd_param_specs(l),
                _const_spec((LANE, SSD_INNER)), _const_spec((SSD_INNER, LANE)), st]
    alias_specs, alias_args, aliases = _alias_args(l, len(in_specs), prev)
    return pl.pallas_call(
        _ssd_dec_kernel,
        grid=(bsz // DEC_BB,),
        in_specs=in_specs + alias_specs,
        out_specs=[pl.BlockSpec((DEC_BB, SSD_INNER), lambda i: (i, 0)), cst, st],
        out_shape=[jax.ShapeDtypeStruct((bsz, SSD_INNER), F32),
                   jax.ShapeDtypeStruct(conv_all.shape, F32),
                   jax.ShapeDtypeStruct(s_all.shape, F32)],
        input_output_aliases=aliases,
        compiler_params=_params(("parallel",)),
        name="ssd_dec",
    )(u, u, dt, conv_all, *_ssd_param_args(p), e, et, s_all, *alias_args)


def _per_head_lanes(x, swap):
    lane = lax.broadcasted_iota(jnp.int32, x.shape, 1)
    first = lane < SWA_HEAD_DIM
    a = jnp.where(first, x, swap)
    b = jnp.where(first, swap, x)
    return jnp.concatenate([a, a, b, b], axis=1)


def _swa_dec_kernel(rel_ref, sink_ref, q_ref, kn_ref, vn_ref, idx_ref, kc_ref, vc_ref, *rest, layer):
    o_ref, ko_ref, vo_ref = rest[-3:]
    bias = _build_bias(rel_ref, idx_ref[...], (8, 2 * LANE))
    rowi = lax.broadcasted_iota(jnp.int32, (SWA_Q_HEADS, 1), 0)
    bias_c = jnp.zeros((SWA_Q_HEADS, WINDOW), F32)
    bias_n = jnp.zeros((SWA_Q_HEADS, 1), F32)
    sink = jnp.zeros((SWA_Q_HEADS, 1), F32)
    for h in range(SWA_Q_HEADS):
        bias_c = jnp.where(rowi == h, bias[h][0:1, :WINDOW], bias_c)
        bias_n = jnp.where(rowi == h, bias[h][0:1, WINDOW:WINDOW + 1], bias_n)
        sink = jnp.where(rowi == h, sink_ref[layer, h], sink)
    own = (lax.broadcasted_iota(jnp.int32, (SWA_Q_HEADS, SWA_QW), 1) // SWA_HEAD_DIM
           == lax.broadcasted_iota(jnp.int32, (SWA_Q_HEADS, SWA_QW), 0))
    scale = SWA_HEAD_DIM ** -0.5
    q = q_ref[...]
    kn = kn_ref[...]
    vn = vn_ref[...]
    knx = _per_head_lanes(kn, pltpu.roll(kn, SWA_HEAD_DIM, 1))
    vnx = _per_head_lanes(vn, pltpu.roll(vn, SWA_HEAD_DIM, 1))
    rows = []
    for bb in range(DEC_BB):
        kc = kc_ref[bb]
        vc = vc_ref[bb]
        ko_ref[bb, 0:WINDOW - 1, :] = kc[1:WINDOW, :]
        ko_ref[bb, WINDOW - 1:WINDOW, :] = kn[bb:bb + 1, :]
        vo_ref[bb, 0:WINDOW - 1, :] = vc[1:WINDOW, :]
        vo_ref[bb, WINDOW - 1:WINDOW, :] = vn[bb:bb + 1, :]
        qx = jnp.where(own, jnp.broadcast_to(q[bb:bb + 1, :], (SWA_Q_HEADS, SWA_QW)), 0.0)
        kx = _per_head_lanes(kc, pltpu.roll(kc, SWA_HEAD_DIM, 1)).astype(BF16)
        vx = _per_head_lanes(vc, pltpu.roll(vc, SWA_HEAD_DIM, 1)).astype(BF16)
        qxb = qx.astype(BF16)
        s_c = _dot_nt(qxb, kx) * scale + bias_c
        knb = knx[bb:bb + 1, :].astype(BF16).astype(F32)
        s_n = jnp.sum(qxb.astype(F32) * knb, axis=-1, keepdims=True) * scale + bias_n
        m = jnp.maximum(jnp.maximum(jnp.max(s_c, axis=-1, keepdims=True), s_n), sink)
        e_c = jnp.exp(s_c - m)
        e_n = jnp.exp(s_n - m)
        den = jnp.sum(e_c, axis=-1, keepdims=True) + e_n + jnp.exp(sink - m)
        pv = _dot((e_c / den).astype(BF16), vx) + (e_n / den) * vnx[bb:bb + 1, :]
        rows.append(jnp.sum(jnp.where(own, pv, 0.0), axis=0, keepdims=True))
    o_ref[...] = jnp.concatenate(rows, axis=0)


def _swa_dec(u, rel_bias, sinks, k_all, v_all, l, prev):
    bsz = u.shape[0]
    dist = np.zeros((8, 2 * LANE), np.int64)
    dist[:, :WINDOW + 1] = WINDOW - np.arange(WINDOW + 1)[None, :]
    idx = jnp.asarray(_t5_bucket_np(dist))
    smem = pl.BlockSpec(memory_space=pltpu.SMEM)
    cache = _state_spec(l, (WINDOW, SWA_KW))
    in_specs = [smem, smem,
                pl.BlockSpec((DEC_BB, SWA_QW), lambda i: (i, OFF_QC // SWA_QW)),
                pl.BlockSpec((DEC_BB, SWA_KW), lambda i: (i, OFF_KC // SWA_KW)),
                pl.BlockSpec((DEC_BB, SWA_KW), lambda i: (i, OFF_VC // SWA_KW)),
                _const_spec((8, 2 * LANE)), cache, cache]
    alias_specs, alias_args, aliases = _alias_args(l, len(in_specs), prev)
    return pl.pallas_call(
        functools.partial(_swa_dec_kernel, layer=l),
        grid=(bsz // DEC_BB,),
        in_specs=in_specs + alias_specs,
        out_specs=[pl.BlockSpec((DEC_BB, SWA_QW), lambda i: (i, 0)), cache, cache],
        out_shape=[jax.ShapeDtypeStruct((bsz, SWA_QW), F32),
                   jax.ShapeDtypeStruct(k_all.shape, F32), jax.ShapeDtypeStruct(v_all.shape, F32)],
        input_output_aliases=aliases,
        compiler_params=_params(("parallel",)),
        name="swa_dec",
    )(rel_bias, sinks, u, u, u, idx, k_all, v_all, *alias_args)


def _rope_tables(pos):
    half = RET_DK // 2
    inv = ROPE_BASE ** (-jnp.arange(half, dtype=F32) / half)
    ang = pos[:, None] * inv[None, :]
    cos = jnp.cos(ang)
    sin = jnp.sin(ang)
    return jnp.concatenate([cos, cos], -1), jnp.concatenate([-sin, sin], -1)


def _reorder_w_in(w):
    sizes = (RET_W, RET_W, RET_W, RET_W, SSD_INNER, SSD_CONV_DIM, SSD_HEADS, SWA_QW, SWA_KW, SWA_KW, 3 * D_MODEL)
    offs = np.concatenate([[0], np.cumsum(sizes)])
    q_r, k_r, v_r, g_r, z, xbc, dt, q_c, k_c, v_c, gates = [w[..., offs[i]:offs[i + 1]] for i in range(len(sizes))]
    pad = jnp.zeros(w.shape[:-1] + (LANE - SSD_HEADS,), w.dtype)
    return jnp.concatenate([gates, q_r, k_r, v_r, g_r, z, xbc, q_c, k_c, v_c, dt, pad], axis=-1).astype(BF16)


def _prepare_params(w_in, conv_w, conv_b, dt_bias, a_log, d_skip, ssd_norm_w, sinks,
                    w_br_ret, w_br_ssd, w_br_swa, w_out, ln1_g, ln1_b, ln2_g, ln2_b,
                    w_ffn_gate, w_ffn_up, w_ffn_down):
    row = lambda v: v.astype(F32)[:, None, :]
    pad = lambda v: jnp.pad(v.astype(F32), ((0, 0), (0, LANE - v.shape[1])))[:, None, :]
    return {
        "w_in": _reorder_w_in(w_in),
        "conv_w": conv_w.astype(F32), "conv_b": row(conv_b),
        "dt_bias": pad(dt_bias), "a_log": pad(a_log),
        "d_skip": row(jnp.repeat(d_skip, SSD_HEAD_DIM, axis=1)),
        "norm_w": row(ssd_norm_w), "sinks": sinks.astype(F32),
        "w_br_ret": w_br_ret.astype(BF16), "w_br_ssd": w_br_ssd.astype(BF16),
        "w_br_swa": w_br_swa.astype(BF16), "w_out": w_out.astype(BF16),
        "ln1_g": row(ln1_g), "ln1_b": row(ln1_b), "ln2_g": row(ln2_g), "ln2_b": row(ln2_b),
        "w_ffn_gate": w_ffn_gate.astype(BF16), "w_ffn_up": w_ffn_up.astype(BF16),
        "w_ffn_down": w_ffn_down.astype(BF16),
    }


def _prompt_layer(x, xb, p, l, rel_bias, cos, sin, bsz, seq):
    m = bsz * seq
    u, dt = _in_proj(xb, p["w_in"], l, min(512, m), BF16)
    o_r, s_ret = _ret_prompt(u, cos, sin, bsz, seq)
    y, s_ssm = _ssd_prompt(u, dt, p, l, bsz, seq)
    o_c = _swa_prompt(u, rel_bias, p["sinks"], l, bsz, seq)
    x1, x1b = _merge(o_r, y, o_c, u, x, p, l, min(512, m))
    x2, x2b = _ffn(x1b, x1, p, l, min(512, m))
    u3 = u.reshape(bsz, seq, N_U)
    conv = u3[:, seq - (SSD_CONV - 1):, OFF_XBC:OFF_XBC + SSD_CONV_DIM].astype(F32)
    k_new = u3[:, seq - WINDOW:, OFF_KC:OFF_KC + SWA_KW].astype(F32)
    v_new = u3[:, seq - WINDOW:, OFF_VC:OFF_VC + SWA_KW].astype(F32)
    return x2, x2b, (s_ret, s_ssm.reshape(bsz, SSD_HEADS, SSD_HEAD_DIM, SSD_STATE), conv,
                     k_new.reshape(bsz, WINDOW, SWA_KV_HEADS, SWA_HEAD_DIM),
                     v_new.reshape(bsz, WINDOW, SWA_KV_HEADS, SWA_HEAD_DIM))


def _sample_layer(x, xb, p, l, rel_bias, cos, sin, states, prev):
    s_ret, s_ssm, s_conv, kbuf, vbuf = states
    bsz = x.shape[0]
    u, dt = _in_proj(xb, p["w_in"], l, bsz, F32)
    pr = prev if l else (None,) * 5
    o_r, ret_new = _ret_dec(u, cos, sin, s_ret, l, pr[0:1])
    y, conv_new, ssm_new = _ssd_dec(u, dt, p, s_conv, s_ssm, l, pr[1:3])
    o_c, k_new, v_new = _swa_dec(u, rel_bias, p["sinks"], kbuf, vbuf, l, pr[3:5])
    x1, x1b = _merge(o_r.astype(BF16), y.astype(BF16), o_c.astype(BF16), u, x, p, l, bsz)
    x2, x2b = _ffn(x1b, x1, p, l, bsz)
    return x2, x2b, (ret_new, conv_new, ssm_new, k_new, v_new)


def kernel(x_prompt, x_sample, state_ret, state_ssm, state_conv, cache_swa_k, cache_swa_v, w_in, conv_w, conv_b, dt_bias, a_log, d_skip, ssd_norm_w, sinks, rel_bias, w_br_ret, w_br_ssd, w_br_swa, w_out, ln1_g, ln1_b, ln2_g, ln2_b, w_ffn_gate, w_ffn_up, w_ffn_down):
    bsz, seq, _ = x_prompt.shape
    dbsz, dseq, _ = x_sample.shape
    assert seq % CHUNK == 0 and dseq == 1 and dbsz % DEC_BB == 0
    p = _prepare_params(w_in, conv_w, conv_b, dt_bias, a_log, d_skip, ssd_norm_w, sinks,
                        w_br_ret, w_br_ssd, w_br_swa, w_out, ln1_g, ln1_b, ln2_g, ln2_b,
                        w_ffn_gate, w_ffn_up, w_ffn_down)
    cos_p, sin_p = _rope_tables(jnp.arange(seq, dtype=F32))
    cos_s, sin_s = _rope_tables(PAST_LEN + jnp.arange(dseq, dtype=F32))
    rel = rel_bias.astype(F32)
    states = (state_ret.astype(F32),
              state_ssm.astype(F32).reshape(DEPTH, dbsz, SSD_INNER, SSD_STATE),
              state_conv.astype(F32).reshape(DEPTH, dbsz, CONV_ST_W),
              cache_swa_k.astype(F32).reshape(DEPTH, dbsz, WINDOW, SWA_KW),
              cache_swa_v.astype(F32).reshape(DEPTH, dbsz, WINDOW, SWA_KW))

    xp = x_prompt.reshape(bsz * seq, D_MODEL).astype(F32)
    xs = x_sample.reshape(dbsz * dseq, D_MODEL).astype(F32)
    xpb, xsb = xp.astype(BF16), xs.astype(BF16)
    new_p = [[] for _ in range(5)]
    new_s = None
    for l in range(DEPTH):
        xp, xpb, st_p = _prompt_layer(xp, xpb, p, l, rel, cos_p, sin_p, bsz, seq)
        xs, xsb, (ret_new, conv_new, ssm_new, k_new, v_new) = _sample_layer(
            xs, xsb, p, l, rel, cos_s, sin_s, states, new_s)
        new_s = (ret_new, conv_new, ssm_new, k_new, v_new)
        for j in range(5):
            new_p[j].append(st_p[j])
    st_p = [jnp.stack(s) for s in new_p]
    ret_s, conv_s, ssm_s, k_s, v_s = new_s
    return (xp.reshape(bsz, seq, D_MODEL), xs.reshape(dbsz, dseq, D_MODEL), *st_p,
            ret_s, ssm_s.reshape(state_ssm.shape), conv_s.reshape(state_conv.shape),
            k_s.reshape(cache_swa_k.shape), v_s.reshape(cache_swa_v.shape))
```

```python
import functools
import math

import jax
import jax.numpy as jnp
import numpy as np
from jax import lax
from jax.experimental import pallas as pl
from jax.experimental.pallas import tpu as pltpu

F32 = jnp.float32
BF16 = jnp.bfloat16

D_MODEL = 1024
DEPTH = 4
PAST_LEN = 8192
RET_HEADS = 4
RET_DK = 128
RET_DV = 128
ROPE_BASE = 10000.0
SSD_HEADS = 16
SSD_HEAD_DIM = 64
SSD_GROUPS = 2
SSD_STATE = 128
SSD_CONV = 4
SSD_INNER = SSD_HEADS * SSD_HEAD_DIM
SSD_CONV_DIM = SSD_INNER + 2 * SSD_GROUPS * SSD_STATE
SWA_Q_HEADS = 8
SWA_KV_HEADS = 2
SWA_HEAD_DIM = 64
WINDOW = 128
N_BUCKETS = 32
BUCKET_MAX_DIST = 128
D_FF = ((8 * D_MODEL // 3 + 255) // 256) * 256
ALPHA = (2 * DEPTH) ** 0.25
EPS = 1e-5
CHUNK = 128

RET_W = RET_HEADS * RET_DK
SWA_QW = SWA_Q_HEADS * SWA_HEAD_DIM
SWA_KW = SWA_KV_HEADS * SWA_HEAD_DIM
CONV_ST_W = (SSD_CONV - 1) * SSD_CONV_DIM

OFF_GATES = 0
OFF_QR = 3 * D_MODEL
OFF_KR = OFF_QR + RET_W
OFF_VR = OFF_KR + RET_W
OFF_GR = OFF_VR + RET_W
OFF_Z = OFF_GR + RET_W
OFF_XBC = OFF_Z + SSD_INNER
OFF_QC = OFF_XBC + SSD_CONV_DIM
OFF_KC = OFF_QC + SWA_QW
OFF_VC = OFF_KC + SWA_KW
N_U = OFF_VC + SWA_KW
LANE = 128
N_PROJ = N_U + LANE
PROJ_CHUNK = 1024

VMEM_LIMIT = 48 * 1024 * 1024
NEG_BIG = -1e30


def _sigmoid(x):
    return 0.5 * jnp.tanh(0.5 * x) + 0.5


def _silu(x):
    h = 0.5 * x
    return h * jnp.tanh(h) + h


def _softplus(x):
    return jnp.maximum(x, 0.0) + jnp.log1p(jnp.exp(-jnp.abs(x)))


def _dot(a, b):
    return jnp.dot(a, b, preferred_element_type=F32)


def _dot_nt(a, b):
    return lax.dot_general(a, b, (((1,), (1,)), ((), ())), preferred_element_type=F32)


def _dot_tn(a, b):
    return lax.dot_general(a, b, (((0,), (0,)), ((), ())), preferred_element_type=F32)


def _split2(x):
    hi = x.astype(BF16)
    lo = (x - hi.astype(F32)).astype(BF16)
    return hi, lo


def _split3(x):
    hi = x.astype(BF16)
    r = x - hi.astype(F32)
    mid = r.astype(BF16)
    lo = (r - mid.astype(F32)).astype(BF16)
    return hi, mid, lo


def _params(sem):
    return pltpu.CompilerParams(dimension_semantics=sem, vmem_limit_bytes=VMEM_LIMIT)


def _layer_spec(l, shape, single_buffer=False):
    nd = len(shape)
    mode = pl.Buffered(1) if single_buffer else None
    return pl.BlockSpec((None,) + tuple(shape), lambda *_: (l,) + (0,) * nd, pipeline_mode=mode)


def _const_spec(shape):
    nd = len(shape)
    return pl.BlockSpec(tuple(shape), lambda *_: (0,) * nd)


_ANY = pl.BlockSpec(memory_space=pl.ANY)


def _in_proj_kernel(x_ref, w_ref, u_ref, dt_ref):
    x = x_ref[...]
    for a in range(0, N_U, PROJ_CHUNK):
        b = min(a + PROJ_CHUNK, N_U)
        u_ref[:, a:b] = _dot(x, w_ref[:, a:b]).astype(u_ref.dtype)
    dt_ref[...] = _dot(x, w_ref[:, N_U:N_PROJ])


def _in_proj(xb, w_all, l, tm, u_dtype):
    m = xb.shape[0]
    return pl.pallas_call(
        _in_proj_kernel,
        grid=(m // tm,),
        in_specs=[pl.BlockSpec((tm, D_MODEL), lambda i: (i, 0)),
                  _layer_spec(l, (D_MODEL, N_PROJ), single_buffer=True)],
        out_specs=[pl.BlockSpec((tm, N_U), lambda i: (i, 0)), pl.BlockSpec((tm, LANE), lambda i: (i, 0))],
        out_shape=[jax.ShapeDtypeStruct((m, N_U), u_dtype), jax.ShapeDtypeStruct((m, LANE), F32)],
        compiler_params=_params(("parallel",)),
        name="in_proj",
    )(xb, w_all)


def _rotary(x, cos, sin):
    return x * cos + pltpu.roll(x, RET_DK // 2, 1) * sin


def _seqs_per_step(bsz):
    return 2 if bsz % 2 == 0 else 1


def _u_spec(nb, off, width):
    return pl.BlockSpec((nb, CHUNK, width), lambda b, c: (b, c, off // width))


def _head_norm(o):
    mu = jnp.mean(o, axis=-1, keepdims=True)
    d = o - mu
    var = jnp.mean(d * d, axis=-1, keepdims=True)
    return d * lax.rsqrt(var + EPS)


def _head_norm_gate(o, g):
    return _head_norm(o) * _silu(g)


def _ret_prompt_kernel(q_ref, k_ref, v_ref, g_ref, cos_ref, sin_ref, dmat_ref, kdec_ref, qdec_ref,
                       o_ref, sfin_ref, s_scr, *, cdec):
    c = pl.program_id(1)

    @pl.when(c == 0)
    def _():
        s_scr[...] = jnp.zeros_like(s_scr)

    cos = cos_ref[...]
    sin = sin_ref[...]
    for bi in range(q_ref.shape[0]):
        for h in range(RET_HEADS):
            sl = slice(h * RET_DK, (h + 1) * RET_DK)
            qr = _rotary(q_ref[bi, :, sl].astype(F32), cos, sin)
            kr = _rotary(k_ref[bi, :, sl].astype(F32), cos, sin) * (RET_DK ** -0.5)
            qb = qr.astype(BF16)
            vb = v_ref[bi, :, sl]
            sc = _dot_nt(qb, kr.astype(BF16)) * dmat_ref[h]
            s_prev = s_scr[bi, h]
            o = _dot(sc.astype(BF16), vb) + _dot(qb, s_prev.astype(BF16)) * qdec_ref[:, sl]
            kd = (kr * kdec_ref[:, sl]).astype(BF16)
            s_scr[bi, h] = cdec[h] * s_prev + _dot_tn(kd, vb)
            o_ref[bi, :, sl] = _head_norm_gate(o, g_ref[bi, :, sl].astype(F32)).astype(o_ref.dtype)

    @pl.when(c == pl.num_programs(1) - 1)
    def _():
        sfin_ref[...] = s_scr[...]


def _ret_consts(c):
    lg = [math.log(1.0 - 2.0 ** (-5.0 - h)) for h in range(RET_HEADS)]
    i = np.arange(c, dtype=np.float64)
    rel = i[:, None] - i[None, :]
    dmat = np.stack([np.where(rel >= 0, np.exp(l * np.maximum(rel, 0.0)), 0.0) for l in lg])
    kdec = np.concatenate([np.repeat(np.exp(l * (c - 1 - i))[:, None], RET_DK, 1) for l in lg], 1)
    qdec = np.concatenate([np.repeat(np.exp(l * (i + 1.0))[:, None], RET_DK, 1) for l in lg], 1)
    cdec = tuple(math.exp(l * c) for l in lg)
    return (jnp.asarray(dmat, F32), jnp.asarray(kdec, F32), jnp.asarray(qdec, F32), cdec)


def _ret_prompt(u3, cos, sin):
    bsz, seq, _ = u3.shape
    nb = _seqs_per_step(bsz)
    dmat, kdec, qdec, cdec = _ret_consts(CHUNK)
    rot = pl.BlockSpec((CHUNK, RET_DK), lambda b, c: (c, 0))
    return pl.pallas_call(
        functools.partial(_ret_prompt_kernel, cdec=cdec),
        grid=(bsz // nb, seq // CHUNK),
        in_specs=[_u_spec(nb, OFF_QR, RET_W), _u_spec(nb, OFF_KR, RET_W), _u_spec(nb, OFF_VR, RET_W),
                  _u_spec(nb, OFF_GR, RET_W), rot, rot,
                  _const_spec((RET_HEADS, CHUNK, CHUNK)), _const_spec((CHUNK, RET_W)),
                  _const_spec((CHUNK, RET_W))],
        out_specs=[pl.BlockSpec((nb, CHUNK, RET_W), lambda b, c: (b, c, 0)),
                   pl.BlockSpec((nb, RET_HEADS, RET_DK, RET_DV), lambda b, c: (b, 0, 0, 0))],
        out_shape=[jax.ShapeDtypeStruct((bsz, seq, RET_W), BF16),
                   jax.ShapeDtypeStruct((bsz, RET_HEADS, RET_DK, RET_DV), F32)],
        scratch_shapes=[pltpu.VMEM((nb, RET_HEADS, RET_DK, RET_DV), F32)],
        compiler_params=_params(("arbitrary", "arbitrary")),
        name="ret_prompt",
    )(u3, u3, u3, u3, cos, sin, dmat, kdec, qdec)


def _expand_heads(v, e_ref):
    hi, lo = _split2(v)
    e = e_ref[...]
    return _dot(hi, e) + _dot(lo, e)


def _group_rms_gate(y, z, w):
    return _group_rms(y * _silu(z), w)


def _group_rms(y, w):
    half = SSD_INNER // SSD_GROUPS
    outs = []
    for g in range(SSD_GROUPS):
        yg = y[:, g * half:(g + 1) * half]
        ms = jnp.mean(yg * yg, axis=-1, keepdims=True)
        outs.append(yg * lax.rsqrt(ms + EPS))
    return jnp.concatenate(outs, axis=-1) * w


CONV_TAIL = 16


def _ssd_prompt_kernel(xbc_ref, z_ref, dt_ref, cw_ref, cb_ref, dtb_ref, alog_ref, dskip_ref, nw_ref,
                       tril_ref, e_ref, shift_ref, shift_tail_ref, y_ref, sfin_ref, tail_scr, st_scr, yi_scr):
    c = pl.program_id(1)

    @pl.when(c == 0)
    def _():
        tail_scr[...] = jnp.zeros_like(tail_scr)
        st_scr[...] = jnp.zeros_like(st_scr)

    taps = [cw_ref[t:t + 1, :].astype(BF16) for t in range(SSD_CONV)]
    a = -jnp.exp(alog_ref[...])
    tril = tril_ref[...]
    ri = lax.broadcasted_iota(jnp.int32, (CHUNK, CHUNK), 0)
    ci = lax.broadcasted_iota(jnp.int32, (CHUNK, CHUNK), 1)
    causal = ri >= ci
    first_head = ci < SSD_HEAD_DIM
    second_head = ci >= SSD_HEAD_DIM
    hg = SSD_HEADS // SSD_GROUPS
    gw = hg * SSD_HEAD_DIM
    for bi in range(xbc_ref.shape[0]):
        xbc = xbc_ref[bi]
        tail = tail_scr[bi]
        prod = jnp.concatenate([xbc * w for w in taps], axis=0)
        prod_tail = jnp.concatenate([tail * w for w in taps[:SSD_CONV - 1]], axis=0)
        conv = _dot(shift_ref[...], prod) + _dot(shift_tail_ref[...], prod_tail) + cb_ref[...]
        tail_scr[bi] = xbc[CHUNK - CONV_TAIL:, :]
        xact = _silu(conv)
        xs = xact[:, :SSD_INNER]
        xs_b = xs.astype(BF16)

        dt = _softplus(dt_ref[bi] + dtb_ref[...])
        d1, d2, d3 = _split3(dt * a)
        cs = _dot(tril, d1) + _dot(tril, d2) + _dot(tril, d3)
        cs_last = cs[CHUNK - 1:CHUNK, :]
        cs_t = cs.T
        dt_t = dt.T
        exp_cs = _expand_heads(jnp.exp(cs), e_ref)
        xw = (xs * _expand_heads(jnp.exp(cs_last - cs) * dt, e_ref)).astype(BF16)

        for g in range(SSD_GROUPS):
            bg = xact[:, SSD_INNER + g * SSD_STATE:SSD_INNER + (g + 1) * SSD_STATE].astype(BF16)
            cg = xact[:, SSD_INNER + (SSD_GROUPS + g) * SSD_STATE:
                      SSD_INNER + (SSD_GROUPS + g + 1) * SSD_STATE].astype(BF16)
            cb = _dot_nt(cg, bg)
            for hh in range(0, hg, 2):
                acc = None
                ps = slice((g * hg + hh) * SSD_HEAD_DIM, (g * hg + hh + 2) * SSD_HEAD_DIM)
                xp = xs_b[:, ps]
                for k in range(2):
                    h = g * hg + hh + k
                    seg = cs[:, h:h + 1] - cs_t[h:h + 1, :]
                    lmat = jnp.exp(jnp.where(causal, seg, NEG_BIG))
                    w = (cb * lmat * dt_t[h:h + 1, :]).astype(BF16)
                    part = _dot(w, jnp.where(first_head if k == 0 else second_head, xp, jnp.zeros_like(xp)))
                    acc = part if acc is None else acc + part
                yi_scr[bi, :, ps] = acc
            gs = slice(g * gw, (g + 1) * gw)
            st_prev = st_scr[bi, :, gs]
            yi_scr[bi, :, gs] = yi_scr[bi, :, gs] + _dot(cg, st_prev.astype(BF16)) * exp_cs[:, gs]
            st_scr[bi, :, gs] = st_prev * exp_cs[CHUNK - 1:CHUNK, gs] + _dot_tn(bg, xw[:, gs])

        y = yi_scr[bi] + dskip_ref[...] * xs
        y_ref[bi] = _group_rms_gate(y, z_ref[bi].astype(F32), nw_ref[...]).astype(y_ref.dtype)

    @pl.when(c == pl.num_programs(1) - 1)
    def _():
        for bi in range(xbc_ref.shape[0]):
            for t in range(SSD_INNER // LANE):
                sfin_ref[bi, t * LANE:(t + 1) * LANE, :] = st_scr[bi, :, t * LANE:(t + 1) * LANE].T


def _ssd_consts():
    tril = np.tril(np.ones((CHUNK, CHUNK), np.float32))
    e = np.zeros((LANE, SSD_INNER), np.float32)
    for h in range(SSD_HEADS):
        e[h, h * SSD_HEAD_DIM:(h + 1) * SSD_HEAD_DIM] = 1.0
    return jnp.asarray(tril, BF16), jnp.asarray(e, BF16)


def _ssd_param_specs(l):
    return [_layer_spec(l, (SSD_CONV, SSD_CONV_DIM)), _layer_spec(l, (1, SSD_CONV_DIM)),
            _layer_spec(l, (1, LANE)), _layer_spec(l, (1, LANE)),
            _layer_spec(l, (1, SSD_INNER)), _layer_spec(l, (1, SSD_INNER))]


def _ssd_param_args(p):
    return (p["conv_w"], p["conv_b"], p["dt_bias"], p["a_log"], p["d_skip"], p["norm_w"])


def _conv_shift_consts():
    k = SSD_CONV
    shift = np.zeros((CHUNK, k * CHUNK), np.float32)
    shift_tail = np.zeros((CHUNK, (k - 1) * CONV_TAIL), np.float32)
    for t in range(k):
        for i in range(CHUNK):
            j = i - (k - 1) + t
            if j >= 0:
                shift[i, t * CHUNK + j] = 1.0
            else:
                shift_tail[i, t * CONV_TAIL + CONV_TAIL + j] = 1.0
    return jnp.asarray(shift, BF16), jnp.asarray(shift_tail, BF16)


def _ssd_prompt(u3, dt3, p, l):
    bsz, seq, _ = u3.shape
    nb = _seqs_per_step(bsz)
    tril, e = _ssd_consts()
    shift, shift_tail = _conv_shift_consts()
    return pl.pallas_call(
        _ssd_prompt_kernel,
        grid=(bsz // nb, seq // CHUNK),
        in_specs=[_u_spec(nb, OFF_XBC, SSD_CONV_DIM), _u_spec(nb, OFF_Z, SSD_INNER),
                  pl.BlockSpec((nb, CHUNK, LANE), lambda b, c: (b, c, 0)),
                  *_ssd_param_specs(l),
                  _const_spec(tril.shape), _const_spec(e.shape), _const_spec(shift.shape),
                  _const_spec(shift_tail.shape)],
        out_specs=[pl.BlockSpec((nb, CHUNK, SSD_INNER), lambda b, c: (b, c, 0)),
                   pl.BlockSpec((nb, SSD_INNER, SSD_STATE), lambda b, c: (b, 0, 0))],
        out_shape=[jax.ShapeDtypeStruct((bsz, seq, SSD_INNER), BF16),
                   jax.ShapeDtypeStruct((bsz, SSD_INNER, SSD_STATE), F32)],
        scratch_shapes=[pltpu.VMEM((nb, CONV_TAIL, SSD_CONV_DIM), BF16),
                        pltpu.VMEM((nb, SSD_STATE, SSD_INNER), F32),
                        pltpu.VMEM((nb, CHUNK, SSD_INNER), F32)],
        compiler_params=_params(("arbitrary", "arbitrary")),
        name="ssd_prompt",
    )(u3, u3, dt3, *_ssd_param_args(p), tril, e, shift, shift_tail)


def _t5_bucket_np(dist):
    max_exact = N_BUCKETS // 2
    df = np.maximum(dist, 1).astype(np.float32)
    large = max_exact + (np.log(df / np.float32(max_exact)) / np.float32(math.log(BUCKET_MAX_DIST / max_exact))
                         * np.float32(N_BUCKETS - max_exact)).astype(np.int32)
    large = np.minimum(large, N_BUCKETS - 1)
    return np.where(dist < max_exact, dist, large).astype(np.int32)


def _build_bias(rel_ref, idx, shape):
    out = []
    for h in range(SWA_Q_HEADS):
        acc = jnp.zeros(shape, F32)
        for b in range(N_BUCKETS):
            acc = jnp.where(idx == b, rel_ref[b, h], acc)
        out.append(acc)
    return out


def _swa_prompt_kernel(rel_ref, sink_ref, q_ref, kp_ref, kc_ref, vp_ref, vc_ref, idx_ref, o_ref, bias_scr,
                       *, layer):
    b = pl.program_id(0)
    n = pl.program_id(1)
    grp = SWA_Q_HEADS // SWA_KV_HEADS

    @pl.when(jnp.logical_and(b == 0, n == 0))
    def _():
        qi = lax.broadcasted_iota(jnp.int32, (CHUNK, 2 * CHUNK), 0)
        kj = lax.broadcasted_iota(jnp.int32, (CHUNK, 2 * CHUNK), 1)
        dist = qi + WINDOW - kj
        band = jnp.logical_and(dist >= 0, dist <= WINDOW)
        for h, bias in enumerate(_build_bias(rel_ref, idx_ref[...], (CHUNK, 2 * CHUNK))):
            g, hh = divmod(h, grp)
            rows = slice(hh * CHUNK, (hh + 1) * CHUNK)
            bias_scr[1, g, rows, :] = jnp.where(band, bias, NEG_BIG)
            bias_scr[0, g, rows, :] = jnp.where(jnp.logical_and(band, kj >= WINDOW), bias, NEG_BIG)

    tbl = jnp.minimum(n, 1)
    gq = grp * SWA_HEAD_DIM
    lane_head = lax.broadcasted_iota(jnp.int32, (CHUNK, gq), 1) // SWA_HEAD_DIM
    for bi in range(q_ref.shape[0]):
        kk = jnp.concatenate([kp_ref[bi], kc_ref[bi]], axis=0)
        vv = jnp.concatenate([vp_ref[bi], vc_ref[bi]], axis=0)
        for g in range(SWA_KV_HEADS):
            ks = slice(g * SWA_HEAD_DIM, (g + 1) * SWA_HEAD_DIM)
            k_rep = jnp.concatenate([kk[:, ks]] * grp, axis=1)
            v_rep = jnp.concatenate([vv[:, ks]] * grp, axis=1)
            qg = q_ref[bi, :, g * gq:(g + 1) * gq] * (SWA_HEAD_DIM ** -0.5)
            q_stack = jnp.concatenate(
                [jnp.where(lane_head == hh, qg, jnp.zeros_like(qg)) for hh in range(grp)], axis=0)
            logits = _dot_nt(q_stack, k_rep) + bias_scr[tbl, g]
            es, rden = [], []
            for hh in range(grp):
                lg = logits[hh * CHUNK:(hh + 1) * CHUNK, :]
                sink = sink_ref[layer, g * grp + hh]
                m = jnp.maximum(jnp.max(lg, axis=-1, keepdims=True), sink)
                e = jnp.exp(lg - m)
                rden.append(1.0 / (jnp.sum(e, axis=-1, keepdims=True) + jnp.exp(sink - m)))
                es.append(e.astype(BF16))
            pv = _dot(jnp.concatenate(es, axis=0), v_rep)
            o = jnp.zeros((CHUNK, gq), F32)
            for hh in range(grp):
                o = jnp.where(lane_head == hh, pv[hh * CHUNK:(hh + 1) * CHUNK, :] * rden[hh], o)
            o_ref[bi, :, g * gq:(g + 1) * gq] = o.astype(o_ref.dtype)


def _swa_prompt(u3, rel_bias, sinks, l):
    bsz, seq, _ = u3.shape
    nb = _seqs_per_step(bsz)
    qi = np.arange(CHUNK)[:, None]
    kj = np.arange(2 * CHUNK)[None, :]
    idx = jnp.asarray(_t5_bucket_np(np.maximum(qi + WINDOW - kj, 0)))
    cur = lambda off, w: pl.BlockSpec((nb, CHUNK, w), lambda b, n: (b, n, off // w))
    prev = lambda off, w: pl.BlockSpec((nb, CHUNK, w), lambda b, n: (b, jnp.maximum(n - 1, 0), off // w))
    smem = pl.BlockSpec(memory_space=pltpu.SMEM)
    grp = SWA_Q_HEADS // SWA_KV_HEADS
    return pl.pallas_call(
        functools.partial(_swa_prompt_kernel, layer=l),
        grid=(bsz // nb, seq // CHUNK),
        in_specs=[smem, smem, cur(OFF_QC, SWA_QW),
                  prev(OFF_KC, SWA_KW), cur(OFF_KC, SWA_KW), prev(OFF_VC, SWA_KW), cur(OFF_VC, SWA_KW),
                  _const_spec((CHUNK, 2 * CHUNK))],
        out_specs=pl.BlockSpec((nb, CHUNK, SWA_QW), lambda b, n: (b, n, 0)),
        out_shape=jax.ShapeDtypeStruct((bsz, seq, SWA_QW), BF16),
        scratch_shapes=[pltpu.VMEM((2, SWA_KV_HEADS, grp * CHUNK, 2 * CHUNK), F32)],
        compiler_params=_params(("arbitrary", "arbitrary")),
        name="swa_prompt",
    )(rel_bias, sinks, u3, u3, u3, u3, u3, idx)


def _layer_norm(r, g, b):
    mu = jnp.mean(r, axis=-1, keepdims=True)
    d = r - mu
    var = jnp.mean(d * d, axis=-1, keepdims=True)
    return d * lax.rsqrt(var + EPS) * g + b


def _merge_kernel(or_ref, y_ref, oc_ref, ga_ref, gb_ref, gc_ref, x_ref, wa_ref, wb_ref, wc_ref, wo_ref,
                  g_ref, b_ref, xo_ref, xob_ref):
    mix = (_sigmoid(ga_ref[...].astype(F32)) * _dot(or_ref[...], wa_ref[...])
           + _sigmoid(gb_ref[...].astype(F32)) * _dot(y_ref[...], wb_ref[...])
           + _sigmoid(gc_ref[...].astype(F32)) * _dot(oc_ref[...], wc_ref[...]))
    r = ALPHA * x_ref[...] + _dot(mix.astype(BF16), wo_ref[...])
    xn = _layer_norm(r, g_ref[...], b_ref[...])
    xo_ref[...] = xn
    xob_ref[...] = xn.astype(BF16)


def _merge(o_r, y, o_c, u, x, p, l, tm):
    m = x.shape[0]
    rows = lambda w: pl.BlockSpec((tm, w), lambda i: (i, 0))
    gate = lambda k: pl.BlockSpec((tm, D_MODEL), lambda i: (i, OFF_GATES // D_MODEL + k))
    wspec = lambda k: _layer_spec(l, (k, D_MODEL), single_buffer=True)
    return pl.pallas_call(
        _merge_kernel,
        grid=(m // tm,),
        in_specs=[rows(RET_W), rows(SSD_INNER), rows(SWA_QW), gate(0), gate(1), gate(2), rows(D_MODEL),
                  wspec(RET_W), wspec(SSD_INNER), wspec(SWA_QW), wspec(D_MODEL),
                  _layer_spec(l, (1, D_MODEL)), _layer_spec(l, (1, D_MODEL))],
        out_specs=[rows(D_MODEL), rows(D_MODEL)],
        out_shape=[jax.ShapeDtypeStruct((m, D_MODEL), F32), jax.ShapeDtypeStruct((m, D_MODEL), BF16)],
        compiler_params=_params(("parallel",)),
        name="merge",
    )(o_r, y, o_c, u, u, u, x, p["w_br_ret"], p["w_br_ssd"], p["w_br_swa"], p["w_out"],
      p["ln1_g"], p["ln1_b"])


def _ffn_kernel(xb_ref, x_ref, wg_ref, wu_ref, wd_ref, g_ref, b_ref, xo_ref, xob_ref):
    xb = xb_ref[...]
    hmid = _silu(_dot(xb, wg_ref[...])) * _dot(xb, wu_ref[...])
    r = ALPHA * x_ref[...] + _dot(hmid.astype(BF16), wd_ref[...])
    xn = _layer_norm(r, g_ref[...], b_ref[...])
    xo_ref[...] = xn
    xob_ref[...] = xn.astype(BF16)


def _ffn(xb, x, p, l, tm):
    m = x.shape[0]
    rows = pl.BlockSpec((tm, D_MODEL), lambda i: (i, 0))
    return pl.pallas_call(
        _ffn_kernel,
        grid=(m // tm,),
        in_specs=[rows, rows,
                  _layer_spec(l, (D_MODEL, D_FF), single_buffer=True),
                  _layer_spec(l, (D_MODEL, D_FF), single_buffer=True),
                  _layer_spec(l, (D_FF, D_MODEL), single_buffer=True),
                  _layer_spec(l, (1, D_MODEL)), _layer_spec(l, (1, D_MODEL))],
        out_specs=[rows, rows],
        out_shape=[jax.ShapeDtypeStruct((m, D_MODEL), F32), jax.ShapeDtypeStruct((m, D_MODEL), BF16)],
        compiler_params=_params(("parallel",)),
        name="ffn",
    )(xb, x, p["w_ffn_gate"], p["w_ffn_up"], p["w_ffn_down"], p["ln2_g"], p["ln2_b"])


DEC_BB = 8


def _state_spec(l, shape):
    nd = len(shape)
    return pl.BlockSpec((None, DEC_BB) + tuple(shape), lambda i: (l, i) + (0,) * nd)


def _alias_args(l, n_in, prev):
    if l == 0:
        return [], [], {}
    return [_ANY] * len(prev), list(prev), {n_in + k: 1 + k for k in range(len(prev))}


def _row_diag(row, eye):
    n = row.shape[1]
    return jnp.where(eye, jnp.broadcast_to(row, (n, n)), 0.0).astype(BF16)


def _ret_dec_kernel(q_ref, k_ref, v_ref, g_ref, cos_ref, sin_ref, s_ref, *rest, gamma):
    o_ref, so_ref = rest[-2:]
    eye = (lax.broadcasted_iota(jnp.int32, (RET_DK, RET_DK), 0)
           == lax.broadcasted_iota(jnp.int32, (RET_DK, RET_DK), 1))
    cos = cos_ref[...]
    sin = sin_ref[...]
    for h in range(RET_HEADS):
        sl = slice(h * RET_DK, (h + 1) * RET_DK)
        qr = _rotary(q_ref[:, sl], cos, sin)
        kr = _rotary(k_ref[:, sl], cos, sin) * (RET_DK ** -0.5)
        v = v_ref[:, sl]
        rows = []
        for bb in range(DEC_BB):
            kv = _dot(_row_diag(kr[bb:bb + 1, :], eye),
                      jnp.broadcast_to(v[bb:bb + 1, :], (RET_DK, RET_DV)).astype(BF16))
            s_new = gamma[h] * s_ref[bb, h] + kv
            so_ref[bb, h] = s_new
            q8 = jnp.broadcast_to(qr[bb:bb + 1, :], (8, RET_DK)).astype(BF16)
            rows.append(_dot(q8, s_new.astype(BF16))[0:1, :])
        o = jnp.concatenate(rows, axis=0)
        o_ref[:, sl] = _head_norm_gate(o, g_ref[:, sl])


def _ret_dec(u, cos, sin, s_all, l, prev):
    bsz = u.shape[0]
    gamma = tuple(1.0 - 2.0 ** (-5.0 - h) for h in range(RET_HEADS))
    usec = lambda off: pl.BlockSpec((DEC_BB, RET_W), lambda i: (i, off // RET_W))
    st = _state_spec(l, (RET_HEADS, RET_DK, RET_DV))
    rot = _const_spec((1, RET_DK))
    in_specs = [usec(OFF_QR), usec(OFF_KR), usec(OFF_VR), usec(OFF_GR), rot, rot, st]
    alias_specs, alias_args, aliases = _alias_args(l, len(in_specs), prev)
    return pl.pallas_call(
        functools.partial(_ret_dec_kernel, gamma=gamma),
        grid=(bsz // DEC_BB,),
        in_specs=in_specs + alias_specs,
        out_specs=[pl.BlockSpec((DEC_BB, RET_W), lambda i: (i, 0)), st],
        out_shape=[jax.ShapeDtypeStruct((bsz, RET_W), F32), jax.ShapeDtypeStruct(s_all.shape, F32)],
        input_output_aliases=aliases,
        compiler_params=_params(("parallel",)),
        name="ret_dec",
    )(u, u, u, u, cos, sin, s_all, *alias_args)


def _ssd_dec_kernel(xn_ref, z_ref, dt_ref, cst_ref, cw_ref, cb_ref, dtb_ref, alog_ref, dskip_ref, nw_ref,
                    e_ref, et_ref, s_ref, *rest):
    y_ref, co_ref, so_ref = rest[-3:]
    w = SSD_CONV_DIM
    xn = xn_ref[...]
    conv = cb_ref[...] + cw_ref[SSD_CONV - 1:SSD_CONV, :] * xn
    for t in range(SSD_CONV - 1):
        conv = conv + cw_ref[t:t + 1, :] * cst_ref[:, t * w:(t + 1) * w]
    for t in range(SSD_CONV - 2):
        co_ref[:, t * w:(t + 1) * w] = cst_ref[:, (t + 1) * w:(t + 2) * w]
    co_ref[:, (SSD_CONV - 2) * w:(SSD_CONV - 1) * w] = xn
    xact = _silu(conv)
    xs = xact[:, :SSD_INNER]

    dt = _softplus(dt_ref[...] + dtb_ref[...])
    decay = jnp.exp(dt * (-jnp.exp(alog_ref[...])))
    dtx = xs * _expand_heads(dt, e_ref)
    et = et_ref[...]
    d1, d2, d3 = _split3(decay)
    dcol = _dot_nt(et, d1) + _dot_nt(et, d2) + _dot_nt(et, d3)

    eye = (lax.broadcasted_iota(jnp.int32, (LANE, LANE), 0) == lax.broadcasted_iota(jnp.int32, (LANE, LANE), 1))
    hg = SSD_HEADS // SSD_GROUPS
    pairs_per_group = hg * SSD_HEAD_DIM // LANE
    rows = []
    for bb in range(DEC_BB):
        pieces = []
        for t in range(SSD_INNER // LANE):
            g = t // pairs_per_group
            ts = slice(t * LANE, (t + 1) * LANE)
            bo = SSD_INNER + g * SSD_STATE
            co = SSD_INNER + (SSD_GROUPS + g) * SSD_STATE
            bmat = jnp.broadcast_to(xact[bb:bb + 1, bo:bo + SSD_STATE], (LANE, SSD_STATE)).astype(BF16)
            upd = _dot(_row_diag(dtx[bb:bb + 1, ts], eye), bmat)
            s_new = dcol[ts, bb:bb + 1] * s_ref[bb, ts, :] + upd
            so_ref[bb, ts, :] = s_new
            c8 = jnp.broadcast_to(xact[bb:bb + 1, co:co + SSD_STATE], (8, SSD_STATE)).astype(BF16)
            pieces.append(_dot_nt(c8, s_new.astype(BF16))[0:1, :])
        rows.append(jnp.concatenate(pieces, axis=1))
    y = jnp.concatenate(rows, axis=0) + dskip_ref[...] * xs
    y_ref[...] = _group_rms_gate(y, z_ref[...], nw_ref[...])


def _ssd_dec(u, dt, p, conv_all, s_all, l, prev):
    bsz = u.shape[0]
    _, e = _ssd_consts()
    et = e.T
    st = _state_spec(l, (SSD_INNER, SSD_STATE))
    cst = _state_spec(l, (CONV_ST_W,))
    in_specs = [pl.BlockSpec((DEC_BB, SSD_CONV_DIM), lambda i: (i, OFF_XBC // SSD_CONV_DIM)),
                pl.BlockSpec((DEC_BB, SSD_INNER), lambda i: (i, OFF_Z // SSD_INNER)),
                pl.BlockSpec((DEC_BB, LANE), lambda i: (i, 0)),
                cst, *_ssd_param_specs(l),
                _const_spec((LANE, SSD_INNER)), _const_spec((SSD_INNER, LANE)), st]
    alias_specs, alias_args, aliases = _alias_args(l, len(in_specs), prev)
    return pl.pallas_call(
        _ssd_dec_kernel,
        grid=(bsz // DEC_BB,),
        in_specs=in_specs + alias_specs,
        out_specs=[pl.BlockSpec((DEC_BB, SSD_INNER), lambda i: (i, 0)), cst, st],
        out_shape=[jax.ShapeDtypeStruct((bsz, SSD_INNER), F32),
                   jax.ShapeDtypeStruct(conv_all.shape, F32),
                   jax.ShapeDtypeStruct(s_all.shape, F32)],
        input_output_aliases=aliases,
        compiler_params=_params(("parallel",)),
        name="ssd_dec",
    )(u, u, dt, conv_all, *_ssd_param_args(p), e, et, s_all, *alias_args)


def _per_head_lanes(x, swap):
    lane = lax.broadcasted_iota(jnp.int32, x.shape, 1)
    first = lane < SWA_HEAD_DIM
    a = jnp.where(first, x, swap)
    b = jnp.where(first, swap, x)
    return jnp.concatenate([a, a, b, b], axis=1)


def _swa_dec_kernel(rel_ref, sink_ref, q_ref, kn_ref, vn_ref, idx_ref, kc_ref, vc_ref, *rest, layer):
    o_ref, ko_ref, vo_ref = rest[-3:]
    bias = _build_bias(rel_ref, idx_ref[...], (8, 2 * LANE))
    rowi = lax.broadcasted_iota(jnp.int32, (SWA_Q_HEADS, 1), 0)
    bias_c = jnp.zeros((SWA_Q_HEADS, WINDOW), F32)
    bias_n = jnp.zeros((SWA_Q_HEADS, 1), F32)
    sink = jnp.zeros((SWA_Q_HEADS, 1), F32)
    for h in range(SWA_Q_HEADS):
        bias_c = jnp.where(rowi == h, bias[h][0:1, :WINDOW], bias_c)
        bias_n = jnp.where(rowi == h, bias[h][0:1, WINDOW:WINDOW + 1], bias_n)
        sink = jnp.where(rowi == h, sink_ref[layer, h], sink)
    own = (lax.broadcasted_iota(jnp.int32, (SWA_Q_HEADS, SWA_QW), 1) // SWA_HEAD_DIM
           == lax.broadcasted_iota(jnp.int32, (SWA_Q_HEADS, SWA_QW), 0))
    scale = SWA_HEAD_DIM ** -0.5
    q = q_ref[...]
    kn = kn_ref[...]
    vn = vn_ref[...]
    knx = _per_head_lanes(kn, pltpu.roll(kn, SWA_HEAD_DIM, 1))
    vnx = _per_head_lanes(vn, pltpu.roll(vn, SWA_HEAD_DIM, 1))
    rows = []
    for bb in range(DEC_BB):
        kc = kc_ref[bb]
        vc = vc_ref[bb]
        ko_ref[bb, 0:WINDOW - 1, :] = kc[1:WINDOW, :]
        ko_ref[bb, WINDOW - 1:WINDOW, :] = kn[bb:bb + 1, :]
        vo_ref[bb, 0:WINDOW - 1, :] = vc[1:WINDOW, :]
        vo_ref[bb, WINDOW - 1:WINDOW, :] = vn[bb:bb + 1, :]
        qx = jnp.where(own, jnp.broadcast_to(q[bb:bb + 1, :], (SWA_Q_HEADS, SWA_QW)), 0.0)
        kx = _per_head_lanes(kc, pltpu.roll(kc, SWA_HEAD_DIM, 1)).astype(BF16)
        vx = _per_head_lanes(vc, pltpu.roll(vc, SWA_HEAD_DIM, 1)).astype(BF16)
        qxb = qx.astype(BF16)
        s_c = _dot_nt(qxb, kx) * scale + bias_c
        knb = knx[bb:bb + 1, :].astype(BF16).astype(F32)
        s_n = jnp.sum(qxb.astype(F32) * knb, axis=-1, keepdims=True) * scale + bias_n
        m = jnp.maximum(jnp.maximum(jnp.max(s_c, axis=-1, keepdims=True), s_n), sink)
        e_c = jnp.exp(s_c - m)
        e_n = jnp.exp(s_n - m)
        den = jnp.sum(e_c, axis=-1, keepdims=True) + e_n + jnp.exp(sink - m)
        pv = _dot((e_c / den).astype(BF16), vx) + (e_n / den) * vnx[bb:bb + 1, :]
        rows.append(jnp.sum(jnp.where(own, pv, 0.0), axis=0, keepdims=True))
    o_ref[...] = jnp.concatenate(rows, axis=0)


def _swa_dec(u, rel_bias, sinks, k_all, v_all, l, prev):
    bsz = u.shape[0]
    dist = np.zeros((8, 2 * LANE), np.int64)
    dist[:, :WINDOW + 1] = WINDOW - np.arange(WINDOW + 1)[None, :]
    idx = jnp.asarray(_t5_bucket_np(dist))
    smem = pl.BlockSpec(memory_space=pltpu.SMEM)
    cache = _state_spec(l, (WINDOW, SWA_KW))
    in_specs = [smem, smem,
                pl.BlockSpec((DEC_BB, SWA_QW), lambda i: (i, OFF_QC // SWA_QW)),
                pl.BlockSpec((DEC_BB, SWA_KW), lambda i: (i, OFF_KC // SWA_KW)),
                pl.BlockSpec((DEC_BB, SWA_KW), lambda i: (i, OFF_VC // SWA_KW)),
                _const_spec((8, 2 * LANE)), cache, cache]
    alias_specs, alias_args, aliases = _alias_args(l, len(in_specs), prev)
    return pl.pallas_call(
        functools.partial(_swa_dec_kernel, layer=l),
        grid=(bsz // DEC_BB,),
        in_specs=in_specs + alias_specs,
        out_specs=[pl.BlockSpec((DEC_BB, SWA_QW), lambda i: (i, 0)), cache, cache],
        out_shape=[jax.ShapeDtypeStruct((bsz, SWA_QW), F32),
                   jax.ShapeDtypeStruct(k_all.shape, F32), jax.ShapeDtypeStruct(v_all.shape, F32)],
        input_output_aliases=aliases,
        compiler_params=_params(("parallel",)),
        name="swa_dec",
    )(rel_bias, sinks, u, u, u, idx, k_all, v_all, *alias_args)


def _rope_tables(pos):
    half = RET_DK // 2
    inv = ROPE_BASE ** (-jnp.arange(half, dtype=F32) / half)
    ang = pos[:, None] * inv[None, :]
    cos = jnp.cos(ang)
    sin = jnp.sin(ang)
    return jnp.concatenate([cos, cos], -1), jnp.concatenate([-sin, sin], -1)


def _reorder_w_in(w):
    sizes = (RET_W, RET_W, RET_W, RET_W, SSD_INNER, SSD_CONV_DIM, SSD_HEADS, SWA_QW, SWA_KW, SWA_KW, 3 * D_MODEL)
    offs = np.concatenate([[0], np.cumsum(sizes)])
    w = w.astype(BF16)
    q_r, k_r, v_r, g_r, z, xbc, dt, q_c, k_c, v_c, gates = [w[..., offs[i]:offs[i + 1]] for i in range(len(sizes))]
    pad = jnp.zeros(w.shape[:-1] + (LANE - SSD_HEADS,), w.dtype)
    return jnp.concatenate([gates, q_r, k_r, v_r, g_r, z, xbc, q_c, k_c, v_c, dt, pad], axis=-1)


def _prepare_params(w_in, conv_w, conv_b, dt_bias, a_log, d_skip, ssd_norm_w, sinks,
                    w_br_ret, w_br_ssd, w_br_swa, w_out, ln1_g, ln1_b, ln2_g, ln2_b,
                    w_ffn_gate, w_ffn_up, w_ffn_down):
    row = lambda v: v.astype(F32)[:, None, :]
    pad = lambda v: jnp.pad(v.astype(F32), ((0, 0), (0, LANE - v.shape[1])))[:, None, :]
    return {
        "w_in": _reorder_w_in(w_in),
        "conv_w": conv_w.astype(F32), "conv_b": row(conv_b),
        "dt_bias": pad(dt_bias), "a_log": pad(a_log),
        "d_skip": row(jnp.repeat(d_skip, SSD_HEAD_DIM, axis=1)),
        "norm_w": row(ssd_norm_w), "sinks": sinks.astype(F32),
        "w_br_ret": w_br_ret.astype(BF16), "w_br_ssd": w_br_ssd.astype(BF16),
        "w_br_swa": w_br_swa.astype(BF16), "w_out": w_out.astype(BF16),
        "ln1_g": row(ln1_g), "ln1_b": row(ln1_b), "ln2_g": row(ln2_g), "ln2_b": row(ln2_b),
        "w_ffn_gate": w_ffn_gate.astype(BF16), "w_ffn_up": w_ffn_up.astype(BF16),
        "w_ffn_down": w_ffn_down.astype(BF16),
    }


def _prompt_layer(x, xb, p, l, rel_bias, cos, sin, bsz, seq):
    m = bsz * seq
    u, dt = _in_proj(xb, p["w_in"], l, min(512, m), BF16)
    u3 = u.reshape(bsz, seq, N_U)
    o_r, s_ret = _ret_prompt(u3, cos, sin)
    y, s_ssm = _ssd_prompt(u3, dt.reshape(bsz, seq, LANE), p, l)
    o_c = _swa_prompt(u3, rel_bias, p["sinks"], l)
    x1, x1b = _merge(o_r.reshape(m, RET_W), y.reshape(m, SSD_INNER), o_c.reshape(m, SWA_QW), u, x, p, l,
                     min(512, m))
    x2, x2b = _ffn(x1b, x1, p, l, min(512, m))
    conv = u3[:, seq - (SSD_CONV - 1):, OFF_XBC:OFF_XBC + SSD_CONV_DIM].astype(F32)
    k_new = u3[:, seq - WINDOW:, OFF_KC:OFF_KC + SWA_KW].astype(F32)
    v_new = u3[:, seq - WINDOW:, OFF_VC:OFF_VC + SWA_KW].astype(F32)
    return x2, x2b, (s_ret, s_ssm.reshape(bsz, SSD_HEADS, SSD_HEAD_DIM, SSD_STATE), conv,
                     k_new.reshape(bsz, WINDOW, SWA_KV_HEADS, SWA_HEAD_DIM),
                     v_new.reshape(bsz, WINDOW, SWA_KV_HEADS, SWA_HEAD_DIM))


def _sample_layer(x, xb, p, l, rel_bias, cos, sin, states, prev):
    s_ret, s_ssm, s_conv, kbuf, vbuf = states
    bsz = x.shape[0]
    u, dt = _in_proj(xb, p["w_in"], l, bsz, F32)
    pr = prev if l else (None,) * 5
    o_r, ret_new = _ret_dec(u, cos, sin, s_ret, l, pr[0:1])
    y, conv_new, ssm_new = _ssd_dec(u, dt, p, s_conv, s_ssm, l, pr[1:3])
    o_c, k_new, v_new = _swa_dec(u, rel_bias, p["sinks"], kbuf, vbuf, l, pr[3:5])
    x1, x1b = _merge(o_r.astype(BF16), y.astype(BF16), o_c.astype(BF16), u, x, p, l, bsz)
    x2, x2b = _ffn(x1b, x1, p, l, bsz)
    return x2, x2b, (ret_new, conv_new, ssm_new, k_new, v_new)


def kernel(x_prompt, x_sample, state_ret, state_ssm, state_conv, cache_swa_k, cache_swa_v, w_in, conv_w, conv_b, dt_bias, a_log, d_skip, ssd_norm_w, sinks, rel_bias, w_br_ret, w_br_ssd, w_br_swa, w_out, ln1_g, ln1_b, ln2_g, ln2_b, w_ffn_gate, w_ffn_up, w_ffn_down):
    bsz, seq, _ = x_prompt.shape
    dbsz, dseq, _ = x_sample.shape
    assert seq % CHUNK == 0 and dseq == 1 and dbsz % DEC_BB == 0
    p = _prepare_params(w_in, conv_w, conv_b, dt_bias, a_log, d_skip, ssd_norm_w, sinks,
                        w_br_ret, w_br_ssd, w_br_swa, w_out, ln1_g, ln1_b, ln2_g, ln2_b,
                        w_ffn_gate, w_ffn_up, w_ffn_down)
    cos_p, sin_p = _rope_tables(jnp.arange(seq, dtype=F32))
    cos_s, sin_s = _rope_tables(PAST_LEN + jnp.arange(dseq, dtype=F32))
    rel = rel_bias.astype(F32)
    states = (state_ret.astype(F32),
              state_ssm.astype(F32).reshape(DEPTH, dbsz, SSD_INNER, SSD_STATE),
              state_conv.astype(F32).reshape(DEPTH, dbsz, CONV_ST_W),
              cache_swa_k.astype(F32).reshape(DEPTH, dbsz, WINDOW, SWA_KW),
              cache_swa_v.astype(F32).reshape(DEPTH, dbsz, WINDOW, SWA_KW))

    xp = x_prompt.reshape(bsz * seq, D_MODEL).astype(F32)
    xs = x_sample.reshape(dbsz * dseq, D_MODEL).astype(F32)
    xpb, xsb = xp.astype(BF16), xs.astype(BF16)
    new_p = [[] for _ in range(5)]
    new_s = None
    for l in range(DEPTH):
        xp, xpb, st_p = _prompt_layer(xp, xpb, p, l, rel, cos_p, sin_p, bsz, seq)
        xs, xsb, (ret_new, conv_new, ssm_new, k_new, v_new) = _sample_layer(
            xs, xsb, p, l, rel, cos_s, sin_s, states, new_s)
        new_s = (ret_new, conv_new, ssm_new, k_new, v_new)
        for j in range(5):
            new_p[j].append(st_p[j])
    st_p = [jnp.stack(s) for s in new_p]
    ret_s, conv_s, ssm_s, k_s, v_s = new_s
    return (xp.reshape(bsz, seq, D_MODEL), xs.reshape(dbsz, dseq, D_MODEL), *st_p,
            ret_s, ssm_s.reshape(state_ssm.shape), conv_s.reshape(state_conv.shape),
            k_s.reshape(cache_swa_k.shape), v_s.reshape(cache_swa_v.shape))
```

```python
import functools
import math

import jax
import jax.numpy as jnp
import numpy as np
from jax import lax
from jax.experimental import pallas as pl
from jax.experimental.pallas import tpu as pltpu

F32 = jnp.float32
BF16 = jnp.bfloat16

D_MODEL = 1024
DEPTH = 4
PAST_LEN = 8192
RET_HEADS = 4
RET_DK = 128
RET_DV = 128
ROPE_BASE = 10000.0
SSD_HEADS = 16
SSD_HEAD_DIM = 64
SSD_GROUPS = 2
SSD_STATE = 128
SSD_CONV = 4
SSD_INNER = SSD_HEADS * SSD_HEAD_DIM
SSD_CONV_DIM = SSD_INNER + 2 * SSD_GROUPS * SSD_STATE
SWA_Q_HEADS = 8
SWA_KV_HEADS = 2
SWA_HEAD_DIM = 64
WINDOW = 128
N_BUCKETS = 32
BUCKET_MAX_DIST = 128
D_FF = ((8 * D_MODEL // 3 + 255) // 256) * 256
ALPHA = (2 * DEPTH) ** 0.25
EPS = 1e-5
CHUNK = 128

RET_W = RET_HEADS * RET_DK
SWA_QW = SWA_Q_HEADS * SWA_HEAD_DIM
SWA_KW = SWA_KV_HEADS * SWA_HEAD_DIM
CONV_ST_W = (SSD_CONV - 1) * SSD_CONV_DIM

OFF_GATES = 0
OFF_QR = 3 * D_MODEL
OFF_KR = OFF_QR + RET_W
OFF_VR = OFF_KR + RET_W
OFF_GR = OFF_VR + RET_W
OFF_Z = OFF_GR + RET_W
OFF_XBC = OFF_Z + SSD_INNER
OFF_QC = OFF_XBC + SSD_CONV_DIM
OFF_KC = OFF_QC + SWA_QW
OFF_VC = OFF_KC + SWA_KW
N_U = OFF_VC + SWA_KW
LANE = 128
N_PROJ = N_U + LANE
PROJ_CHUNK = 1024

VMEM_LIMIT = 48 * 1024 * 1024
NEG_BIG = -1e30


def _sigmoid(x):
    return 0.5 * jnp.tanh(0.5 * x) + 0.5


def _silu(x):
    h = 0.5 * x
    return h * jnp.tanh(h) + h


def _softplus(x):
    return jnp.maximum(x, 0.0) + jnp.log1p(jnp.exp(-jnp.abs(x)))


def _dot(a, b):
    return jnp.dot(a, b, preferred_element_type=F32)


def _dot_nt(a, b):
    return lax.dot_general(a, b, (((1,), (1,)), ((), ())), preferred_element_type=F32)


def _dot_tn(a, b):
    return lax.dot_general(a, b, (((0,), (0,)), ((), ())), preferred_element_type=F32)


def _split2(x):
    hi = x.astype(BF16)
    lo = (x - hi.astype(F32)).astype(BF16)
    return hi, lo


def _split3(x):
    hi = x.astype(BF16)
    r = x - hi.astype(F32)
    mid = r.astype(BF16)
    lo = (r - mid.astype(F32)).astype(BF16)
    return hi, mid, lo


def _params(sem):
    return pltpu.CompilerParams(dimension_semantics=sem, vmem_limit_bytes=VMEM_LIMIT)


def _layer_spec(l, shape, single_buffer=False):
    nd = len(shape)
    mode = pl.Buffered(1) if single_buffer else None
    return pl.BlockSpec((None,) + tuple(shape), lambda *_: (l,) + (0,) * nd, pipeline_mode=mode)


def _const_spec(shape):
    nd = len(shape)
    return pl.BlockSpec(tuple(shape), lambda *_: (0,) * nd)


_ANY = pl.BlockSpec(memory_space=pl.ANY)


def _in_proj_kernel(x_ref, w_ref, u_ref, dt_ref):
    x = x_ref[...]
    for a in range(0, N_U, PROJ_CHUNK):
        b = min(a + PROJ_CHUNK, N_U)
        u_ref[:, a:b] = _dot(x, w_ref[:, a:b]).astype(u_ref.dtype)
    dt_ref[...] = _dot(x, w_ref[:, N_U:N_PROJ])


def _in_proj(xb, w_all, l, tm, u_dtype):
    m = xb.shape[0]
    return pl.pallas_call(
        _in_proj_kernel,
        grid=(m // tm,),
        in_specs=[pl.BlockSpec((tm, D_MODEL), lambda i: (i, 0)),
                  _layer_spec(l, (D_MODEL, N_PROJ), single_buffer=True)],
        out_specs=[pl.BlockSpec((tm, N_U), lambda i: (i, 0)), pl.BlockSpec((tm, LANE), lambda i: (i, 0))],
        out_shape=[jax.ShapeDtypeStruct((m, N_U), u_dtype), jax.ShapeDtypeStruct((m, LANE), F32)],
        compiler_params=_params(("parallel",)),
        name="in_proj",
    )(xb, w_all)


def _rotary(x, cos, sin):
    return x * cos + pltpu.roll(x, RET_DK // 2, 1) * sin


def _seqs_per_step(bsz):
    return next(n for n in (4, 2, 1) if bsz % n == 0)


def _u_spec(nb, off, width):
    return pl.BlockSpec((nb, CHUNK, width), lambda b, c: (b, c, off // width))


def _head_norm(o):
    mu = jnp.mean(o, axis=-1, keepdims=True)
    d = o - mu
    var = jnp.mean(d * d, axis=-1, keepdims=True)
    return d * lax.rsqrt(var + EPS)


def _head_norm_gate(o, g):
    return _head_norm(o) * _silu(g)


def _ret_prompt_kernel(q_ref, k_ref, v_ref, g_ref, cos_ref, sin_ref, dmat_ref, kdec_ref, qdec_ref,
                       o_ref, sfin_ref, s_scr, *, cdec):
    c = pl.program_id(1)

    @pl.when(c == 0)
    def _():
        s_scr[...] = jnp.zeros_like(s_scr)

    cos = cos_ref[...]
    sin = sin_ref[...]
    for bi in range(q_ref.shape[0]):
        for h in range(RET_HEADS):
            sl = slice(h * RET_DK, (h + 1) * RET_DK)
            qr = _rotary(q_ref[bi, :, sl].astype(F32), cos, sin)
            kr = _rotary(k_ref[bi, :, sl].astype(F32), cos, sin) * (RET_DK ** -0.5)
            qb = qr.astype(BF16)
            vb = v_ref[bi, :, sl]
            sc = _dot_nt(qb, kr.astype(BF16)) * dmat_ref[h]
            s_prev = s_scr[bi, h]
            o = _dot(sc.astype(BF16), vb) + _dot(qb, s_prev.astype(BF16)) * qdec_ref[:, sl]
            kd = (kr * kdec_ref[:, sl]).astype(BF16)
            s_scr[bi, h] = cdec[h] * s_prev + _dot_tn(kd, vb)
            o_ref[bi, :, sl] = _head_norm_gate(o, g_ref[bi, :, sl].astype(F32)).astype(o_ref.dtype)

    @pl.when(c == pl.num_programs(1) - 1)
    def _():
        sfin_ref[...] = s_scr[...]


def _ret_consts(c):
    lg = [math.log(1.0 - 2.0 ** (-5.0 - h)) for h in range(RET_HEADS)]
    i = np.arange(c, dtype=np.float64)
    rel = i[:, None] - i[None, :]
    dmat = np.stack([np.where(rel >= 0, np.exp(l * np.maximum(rel, 0.0)), 0.0) for l in lg])
    kdec = np.concatenate([np.repeat(np.exp(l * (c - 1 - i))[:, None], RET_DK, 1) for l in lg], 1)
    qdec = np.concatenate([np.repeat(np.exp(l * (i + 1.0))[:, None], RET_DK, 1) for l in lg], 1)
    cdec = tuple(math.exp(l * c) for l in lg)
    return (jnp.asarray(dmat, F32), jnp.asarray(kdec, F32), jnp.asarray(qdec, F32), cdec)


def _ret_prompt(u3, cos, sin):
    bsz, seq, _ = u3.shape
    nb = _seqs_per_step(bsz)
    dmat, kdec, qdec, cdec = _ret_consts(CHUNK)
    rot = pl.BlockSpec((CHUNK, RET_DK), lambda b, c: (c, 0))
    return pl.pallas_call(
        functools.partial(_ret_prompt_kernel, cdec=cdec),
        grid=(bsz // nb, seq // CHUNK),
        in_specs=[_u_spec(nb, OFF_QR, RET_W), _u_spec(nb, OFF_KR, RET_W), _u_spec(nb, OFF_VR, RET_W),
                  _u_spec(nb, OFF_GR, RET_W), rot, rot,
                  _const_spec((RET_HEADS, CHUNK, CHUNK)), _const_spec((CHUNK, RET_W)),
                  _const_spec((CHUNK, RET_W))],
        out_specs=[pl.BlockSpec((nb, CHUNK, RET_W), lambda b, c: (b, c, 0)),
                   pl.BlockSpec((nb, RET_HEADS, RET_DK, RET_DV), lambda b, c: (b, 0, 0, 0))],
        out_shape=[jax.ShapeDtypeStruct((bsz, seq, RET_W), BF16),
                   jax.ShapeDtypeStruct((bsz, RET_HEADS, RET_DK, RET_DV), F32)],
        scratch_shapes=[pltpu.VMEM((nb, RET_HEADS, RET_DK, RET_DV), F32)],
        compiler_params=_params(("arbitrary", "arbitrary")),
        name="ret_prompt",
    )(u3, u3, u3, u3, cos, sin, dmat, kdec, qdec)


def _expand_heads(v, e_ref):
    hi, lo = _split2(v)
    e = e_ref[...]
    return _dot(hi, e) + _dot(lo, e)


def _group_rms_gate(y, z, w):
    return _group_rms(y * _silu(z), w)


def _group_rms(y, w):
    half = SSD_INNER // SSD_GROUPS
    outs = []
    for g in range(SSD_GROUPS):
        yg = y[:, g * half:(g + 1) * half]
        ms = jnp.mean(yg * yg, axis=-1, keepdims=True)
        outs.append(yg * lax.rsqrt(ms + EPS))
    return jnp.concatenate(outs, axis=-1) * w


CONV_TAIL = 16


def _ssd_prompt_kernel(xbc_ref, z_ref, dt_ref, cw_ref, cb_ref, dtb_ref, alog_ref, dskip_ref, nw_ref,
                       tril_ref, e_ref, shift_ref, shift_tail_ref, y_ref, sfin_ref, tail_scr, st_scr, yi_scr):
    c = pl.program_id(1)

    @pl.when(c == 0)
    def _():
        tail_scr[...] = jnp.zeros_like(tail_scr)
        st_scr[...] = jnp.zeros_like(st_scr)

    taps = [cw_ref[t:t + 1, :].astype(BF16) for t in range(SSD_CONV)]
    a = -jnp.exp(alog_ref[...])
    tril = tril_ref[...]
    ri = lax.broadcasted_iota(jnp.int32, (CHUNK, CHUNK), 0)
    ci = lax.broadcasted_iota(jnp.int32, (CHUNK, CHUNK), 1)
    causal = ri >= ci
    first_head = ci < SSD_HEAD_DIM
    second_head = ci >= SSD_HEAD_DIM
    hg = SSD_HEADS // SSD_GROUPS
    gw = hg * SSD_HEAD_DIM
    for bi in range(xbc_ref.shape[0]):
        xbc = xbc_ref[bi]
        tail = tail_scr[bi]
        prod = jnp.concatenate([xbc * w for w in taps], axis=0)
        prod_tail = jnp.concatenate([tail * w for w in taps[:SSD_CONV - 1]], axis=0)
        conv = _dot(shift_ref[...], prod) + _dot(shift_tail_ref[...], prod_tail) + cb_ref[...]
        tail_scr[bi] = xbc[CHUNK - CONV_TAIL:, :]
        xact = _silu(conv)
        xs = xact[:, :SSD_INNER]
        xs_b = xs.astype(BF16)

        dt = _softplus(dt_ref[bi] + dtb_ref[...])
        d1, d2, d3 = _split3(dt * a)
        cs = _dot(tril, d1) + _dot(tril, d2) + _dot(tril, d3)
        cs_last = cs[CHUNK - 1:CHUNK, :]
        cs_t = cs.T
        dt_t = dt.T
        exp_cs = _expand_heads(jnp.exp(cs), e_ref)
        xw = (xs * _expand_heads(jnp.exp(cs_last - cs) * dt, e_ref)).astype(BF16)

        for g in range(SSD_GROUPS):
            bg = xact[:, SSD_INNER + g * SSD_STATE:SSD_INNER + (g + 1) * SSD_STATE].astype(BF16)
            cg = xact[:, SSD_INNER + (SSD_GROUPS + g) * SSD_STATE:
                      SSD_INNER + (SSD_GROUPS + g + 1) * SSD_STATE].astype(BF16)
            cb = _dot_nt(cg, bg)
            for hh in range(0, hg, 2):
                acc = None
                ps = slice((g * hg + hh) * SSD_HEAD_DIM, (g * hg + hh + 2) * SSD_HEAD_DIM)
                xp = xs_b[:, ps]
                for k in range(2):
                    h = g * hg + hh + k
                    seg = cs[:, h:h + 1] - cs_t[h:h + 1, :]
                    lmat = jnp.exp(jnp.where(causal, seg, NEG_BIG))
                    w = (cb * lmat * dt_t[h:h + 1, :]).astype(BF16)
                    part = _dot(w, jnp.where(first_head if k == 0 else second_head, xp, jnp.zeros_like(xp)))
                    acc = part if acc is None else acc + part
                yi_scr[bi, :, ps] = acc
            gs = slice(g * gw, (g + 1) * gw)
            st_prev = st_scr[bi, :, gs]
            yi_scr[bi, :, gs] = yi_scr[bi, :, gs] + _dot(cg, st_prev.astype(BF16)) * exp_cs[:, gs]
            st_scr[bi, :, gs] = st_prev * exp_cs[CHUNK - 1:CHUNK, gs] + _dot_tn(bg, xw[:, gs])

        y = yi_scr[bi] + dskip_ref[...] * xs
        y_ref[bi] = _group_rms_gate(y, z_ref[bi].astype(F32), nw_ref[...]).astype(y_ref.dtype)

    @pl.when(c == pl.num_programs(1) - 1)
    def _():
        for bi in range(xbc_ref.shape[0]):
            for t in range(SSD_INNER // LANE):
                sfin_ref[bi, t * LANE:(t + 1) * LANE, :] = st_scr[bi, :, t * LANE:(t + 1) * LANE].T


def _ssd_consts():
    tril = np.tril(np.ones((CHUNK, CHUNK), np.float32))
    e = np.zeros((LANE, SSD_INNER), np.float32)
    for h in range(SSD_HEADS):
        e[h, h * SSD_HEAD_DIM:(h + 1) * SSD_HEAD_DIM] = 1.0
    return jnp.asarray(tril, BF16), jnp.asarray(e, BF16)


def _ssd_param_specs(l):
    return [_layer_spec(l, (SSD_CONV, SSD_CONV_DIM)), _layer_spec(l, (1, SSD_CONV_DIM)),
            _layer_spec(l, (1, LANE)), _layer_spec(l, (1, LANE)),
            _layer_spec(l, (1, SSD_INNER)), _layer_spec(l, (1, SSD_INNER))]


def _ssd_param_args(p):
    return (p["conv_w"], p["conv_b"], p["dt_bias"], p["a_log"], p["d_skip"], p["norm_w"])


def _conv_shift_consts():
    k = SSD_CONV
    shift = np.zeros((CHUNK, k * CHUNK), np.float32)
    shift_tail = np.zeros((CHUNK, (k - 1) * CONV_TAIL), np.float32)
    for t in range(k):
        for i in range(CHUNK):
            j = i - (k - 1) + t
            if j >= 0:
                shift[i, t * CHUNK + j] = 1.0
            else:
                shift_tail[i, t * CONV_TAIL + CONV_TAIL + j] = 1.0
    return jnp.asarray(shift, BF16), jnp.asarray(shift_tail, BF16)


def _ssd_prompt(u3, dt3, p, l):
    bsz, seq, _ = u3.shape
    nb = _seqs_per_step(bsz)
    tril, e = _ssd_consts()
    shift, shift_tail = _conv_shift_consts()
    return pl.pallas_call(
        _ssd_prompt_kernel,
        grid=(bsz // nb, seq // CHUNK),
        in_specs=[_u_spec(nb, OFF_XBC, SSD_CONV_DIM), _u_spec(nb, OFF_Z, SSD_INNER),
                  pl.BlockSpec((nb, CHUNK, LANE), lambda b, c: (b, c, 0)),
                  *_ssd_param_specs(l),
                  _const_spec(tril.shape), _const_spec(e.shape), _const_spec(shift.shape),
                  _const_spec(shift_tail.shape)],
        out_specs=[pl.BlockSpec((nb, CHUNK, SSD_INNER), lambda b, c: (b, c, 0)),
                   pl.BlockSpec((nb, SSD_INNER, SSD_STATE), lambda b, c: (b, 0, 0))],
        out_shape=[jax.ShapeDtypeStruct((bsz, seq, SSD_INNER), BF16),
                   jax.ShapeDtypeStruct((bsz, SSD_INNER, SSD_STATE), F32)],
        scratch_shapes=[pltpu.VMEM((nb, CONV_TAIL, SSD_CONV_DIM), BF16),
                        pltpu.VMEM((nb, SSD_STATE, SSD_INNER), F32),
                        pltpu.VMEM((nb, CHUNK, SSD_INNER), F32)],
        compiler_params=_params(("arbitrary", "arbitrary")),
        name="ssd_prompt",
    )(u3, u3, dt3, *_ssd_param_args(p), tril, e, shift, shift_tail)


def _t5_bucket_np(dist):
    max_exact = N_BUCKETS // 2
    df = np.maximum(dist, 1).astype(np.float32)
    large = max_exact + (np.log(df / np.float32(max_exact)) / np.float32(math.log(BUCKET_MAX_DIST / max_exact))
                         * np.float32(N_BUCKETS - max_exact)).astype(np.int32)
    large = np.minimum(large, N_BUCKETS - 1)
    return np.where(dist < max_exact, dist, large).astype(np.int32)


def _build_bias(rel_ref, idx, shape):
    out = []
    for h in range(SWA_Q_HEADS):
        acc = jnp.zeros(shape, F32)
        for b in range(N_BUCKETS):
            acc = jnp.where(idx == b, rel_ref[b, h], acc)
        out.append(acc)
    return out


def _swa_prompt_kernel(rel_ref, sink_ref, q_ref, kp_ref, kc_ref, vp_ref, vc_ref, idx_ref, o_ref, bias_scr,
                       *, layer):
    b = pl.program_id(0)
    n = pl.program_id(1)
    grp = SWA_Q_HEADS // SWA_KV_HEADS

    @pl.when(jnp.logical_and(b == 0, n == 0))
    def _():
        qi = lax.broadcasted_iota(jnp.int32, (CHUNK, 2 * CHUNK), 0)
        kj = lax.broadcasted_iota(jnp.int32, (CHUNK, 2 * CHUNK), 1)
        dist = qi + WINDOW - kj
        band = jnp.logical_and(dist >= 0, dist <= WINDOW)
        for h, bias in enumerate(_build_bias(rel_ref, idx_ref[...], (CHUNK, 2 * CHUNK))):
            g, hh = divmod(h, grp)
            rows = slice(hh * CHUNK, (hh + 1) * CHUNK)
            bias_scr[1, g, rows, :] = jnp.where(band, bias, NEG_BIG)
            bias_scr[0, g, rows, :] = jnp.where(jnp.logical_and(band, kj >= WINDOW), bias, NEG_BIG)

    tbl = jnp.minimum(n, 1)
    gq = grp * SWA_HEAD_DIM
    lane_head = lax.broadcasted_iota(jnp.int32, (CHUNK, gq), 1) // SWA_HEAD_DIM
    for bi in range(q_ref.shape[0]):
        kk = jnp.concatenate([kp_ref[bi], kc_ref[bi]], axis=0)
        vv = jnp.concatenate([vp_ref[bi], vc_ref[bi]], axis=0)
        for g in range(SWA_KV_HEADS):
            ks = slice(g * SWA_HEAD_DIM, (g + 1) * SWA_HEAD_DIM)
            k_rep = jnp.concatenate([kk[:, ks]] * grp, axis=1)
            v_rep = jnp.concatenate([vv[:, ks]] * grp, axis=1)
            qg = q_ref[bi, :, g * gq:(g + 1) * gq] * (SWA_HEAD_DIM ** -0.5)
            q_stack = jnp.concatenate(
                [jnp.where(lane_head == hh, qg, jnp.zeros_like(qg)) for hh in range(grp)], axis=0)
            logits = _dot_nt(q_stack, k_rep) + bias_scr[tbl, g]
            es, rden = [], []
            for hh in range(grp):
                lg = logits[hh * CHUNK:(hh + 1) * CHUNK, :]
                sink = sink_ref[layer, g * grp + hh]
                m = jnp.maximum(jnp.max(lg, axis=-1, keepdims=True), sink)
                e = jnp.exp(lg - m)
                rden.append(1.0 / (jnp.sum(e, axis=-1, keepdims=True) + jnp.exp(sink - m)))
                es.append(e.astype(BF16))
            pv = _dot(jnp.concatenate(es, axis=0), v_rep)
            o = jnp.zeros((CHUNK, gq), F32)
            for hh in range(grp):
                o = jnp.where(lane_head == hh, pv[hh * CHUNK:(hh + 1) * CHUNK, :] * rden[hh], o)
            o_ref[bi, :, g * gq:(g + 1) * gq] = o.astype(o_ref.dtype)


def _swa_prompt(u3, rel_bias, sinks, l):
    bsz, seq, _ = u3.shape
    nb = _seqs_per_step(bsz)
    qi = np.arange(CHUNK)[:, None]
    kj = np.arange(2 * CHUNK)[None, :]
    idx = jnp.asarray(_t5_bucket_np(np.maximum(qi + WINDOW - kj, 0)))
    cur = lambda off, w: pl.BlockSpec((nb, CHUNK, w), lambda b, n: (b, n, off // w))
    prev = lambda off, w: pl.BlockSpec((nb, CHUNK, w), lambda b, n: (b, jnp.maximum(n - 1, 0), off // w))
    smem = pl.BlockSpec(memory_space=pltpu.SMEM)
    grp = SWA_Q_HEADS // SWA_KV_HEADS
    return pl.pallas_call(
        functools.partial(_swa_prompt_kernel, layer=l),
        grid=(bsz // nb, seq // CHUNK),
        in_specs=[smem, smem, cur(OFF_QC, SWA_QW),
                  prev(OFF_KC, SWA_KW), cur(OFF_KC, SWA_KW), prev(OFF_VC, SWA_KW), cur(OFF_VC, SWA_KW),
                  _const_spec((CHUNK, 2 * CHUNK))],
        out_specs=pl.BlockSpec((nb, CHUNK, SWA_QW), lambda b, n: (b, n, 0)),
        out_shape=jax.ShapeDtypeStruct((bsz, seq, SWA_QW), BF16),
        scratch_shapes=[pltpu.VMEM((2, SWA_KV_HEADS, grp * CHUNK, 2 * CHUNK), F32)],
        compiler_params=_params(("arbitrary", "arbitrary")),
        name="swa_prompt",
    )(rel_bias, sinks, u3, u3, u3, u3, u3, idx)


def _layer_norm(r, g, b):
    mu = jnp.mean(r, axis=-1, keepdims=True)
    d = r - mu
    var = jnp.mean(d * d, axis=-1, keepdims=True)
    return d * lax.rsqrt(var + EPS) * g + b


def _merge_kernel(or_ref, y_ref, oc_ref, ga_ref, gb_ref, gc_ref, x_ref, wa_ref, wb_ref, wc_ref, wo_ref,
                  g_ref, b_ref, xo_ref, xob_ref):
    mix = (_sigmoid(ga_ref[...].astype(F32)) * _dot(or_ref[...], wa_ref[...])
           + _sigmoid(gb_ref[...].astype(F32)) * _dot(y_ref[...], wb_ref[...])
           + _sigmoid(gc_ref[...].astype(F32)) * _dot(oc_ref[...], wc_ref[...]))
    r = ALPHA * x_ref[...] + _dot(mix.astype(BF16), wo_ref[...])
    xn = _layer_norm(r, g_ref[...], b_ref[...])
    xo_ref[...] = xn
    xob_ref[...] = xn.astype(BF16)


def _merge(o_r, y, o_c, u, x, p, l, tm):
    m = x.shape[0]
    rows = lambda w: pl.BlockSpec((tm, w), lambda i: (i, 0))
    gate = lambda k: pl.BlockSpec((tm, D_MODEL), lambda i: (i, OFF_GATES // D_MODEL + k))
    wspec = lambda k: _layer_spec(l, (k, D_MODEL), single_buffer=True)
    return pl.pallas_call(
        _merge_kernel,
        grid=(m // tm,),
        in_specs=[rows(RET_W), rows(SSD_INNER), rows(SWA_QW), gate(0), gate(1), gate(2), rows(D_MODEL),
                  wspec(RET_W), wspec(SSD_INNER), wspec(SWA_QW), wspec(D_MODEL),
                  _layer_spec(l, (1, D_MODEL)), _layer_spec(l, (1, D_MODEL))],
        out_specs=[rows(D_MODEL), rows(D_MODEL)],
        out_shape=[jax.ShapeDtypeStruct((m, D_MODEL), F32), jax.ShapeDtypeStruct((m, D_MODEL), BF16)],
        compiler_params=_params(("parallel",)),
        name="merge",
    )(o_r, y, o_c, u, u, u, x, p["w_br_ret"], p["w_br_ssd"], p["w_br_swa"], p["w_out"],
      p["ln1_g"], p["ln1_b"])


def _ffn_kernel(xb_ref, x_ref, wg_ref, wu_ref, wd_ref, g_ref, b_ref, xo_ref, xob_ref):
    xb = xb_ref[...]
    hmid = _silu(_dot(xb, wg_ref[...])) * _dot(xb, wu_ref[...])
    r = ALPHA * x_ref[...] + _dot(hmid.astype(BF16), wd_ref[...])
    xn = _layer_norm(r, g_ref[...], b_ref[...])
    xo_ref[...] = xn
    xob_ref[...] = xn.astype(BF16)


def _ffn(xb, x, p, l, tm):
    m = x.shape[0]
    rows = pl.BlockSpec((tm, D_MODEL), lambda i: (i, 0))
    return pl.pallas_call(
        _ffn_kernel,
        grid=(m // tm,),
        in_specs=[rows, rows,
                  _layer_spec(l, (D_MODEL, D_FF), single_buffer=True),
                  _layer_spec(l, (D_MODEL, D_FF), single_buffer=True),
                  _layer_spec(l, (D_FF, D_MODEL), single_buffer=True),
                  _layer_spec(l, (1, D_MODEL)), _layer_spec(l, (1, D_MODEL))],
        out_specs=[rows, rows],
        out_shape=[jax.ShapeDtypeStruct((m, D_MODEL), F32), jax.ShapeDtypeStruct((m, D_MODEL), BF16)],
        compiler_params=_params(("parallel",)),
        name="ffn",
    )(xb, x, p["w_ffn_gate"], p["w_ffn_up"], p["w_ffn_down"], p["ln2_g"], p["ln2_b"])


DEC_BB = 8


def _state_spec(l, shape):
    nd = len(shape)
    return pl.BlockSpec((None, DEC_BB) + tuple(shape), lambda i: (l, i) + (0,) * nd)


def _alias_args(l, n_in, prev):
    if l == 0:
        return [], [], {}
    return [_ANY] * len(prev), list(prev), {n_in + k: 1 + k for k in range(len(prev))}


def _row_diag(row, eye):
    n = row.shape[1]
    return jnp.where(eye, jnp.broadcast_to(row, (n, n)), 0.0).astype(BF16)


def _ret_dec_kernel(q_ref, k_ref, v_ref, g_ref, cos_ref, sin_ref, s_ref, *rest, gamma):
    o_ref, so_ref = rest[-2:]
    eye = (lax.broadcasted_iota(jnp.int32, (RET_DK, RET_DK), 0)
           == lax.broadcasted_iota(jnp.int32, (RET_DK, RET_DK), 1))
    cos = cos_ref[...]
    sin = sin_ref[...]
    for h in range(RET_HEADS):
        sl = slice(h * RET_DK, (h + 1) * RET_DK)
        qr = _rotary(q_ref[:, sl], cos, sin)
        kr = _rotary(k_ref[:, sl], cos, sin) * (RET_DK ** -0.5)
        v = v_ref[:, sl]
        rows = []
        for bb in range(DEC_BB):
            kv = _dot(_row_diag(kr[bb:bb + 1, :], eye),
                      jnp.broadcast_to(v[bb:bb + 1, :], (RET_DK, RET_DV)).astype(BF16))
            s_new = gamma[h] * s_ref[bb, h] + kv
            so_ref[bb, h] = s_new
            q8 = jnp.broadcast_to(qr[bb:bb + 1, :], (8, RET_DK)).astype(BF16)
            rows.append(_dot(q8, s_new.astype(BF16))[0:1, :])
        o = jnp.concatenate(rows, axis=0)
        o_ref[:, sl] = _head_norm_gate(o, g_ref[:, sl])


def _ret_dec(u, cos, sin, s_all, l, prev):
    bsz = u.shape[0]
    gamma = tuple(1.0 - 2.0 ** (-5.0 - h) for h in range(RET_HEADS))
    usec = lambda off: pl.BlockSpec((DEC_BB, RET_W), lambda i: (i, off // RET_W))
    st = _state_spec(l, (RET_HEADS, RET_DK, RET_DV))
    rot = _const_spec((1, RET_DK))
    in_specs = [usec(OFF_QR), usec(OFF_KR), usec(OFF_VR), usec(OFF_GR), rot, rot, st]
    alias_specs, alias_args, aliases = _alias_args(l, len(in_specs), prev)
    return pl.pallas_call(
        functools.partial(_ret_dec_kernel, gamma=gamma),
        grid=(bsz // DEC_BB,),
        in_specs=in_specs + alias_specs,
        out_specs=[pl.BlockSpec((DEC_BB, RET_W), lambda i: (i, 0)), st],
        out_shape=[jax.ShapeDtypeStruct((bsz, RET_W), F32), jax.ShapeDtypeStruct(s_all.shape, F32)],
        input_output_aliases=aliases,
        compiler_params=_params(("parallel",)),
        name="ret_dec",
    )(u, u, u, u, cos, sin, s_all, *alias_args)


def _ssd_dec_kernel(xn_ref, z_ref, dt_ref, cst_ref, cw_ref, cb_ref, dtb_ref, alog_ref, dskip_ref, nw_ref,
                    e_ref, et_ref, s_ref, *rest):
    y_ref, co_ref, so_ref = rest[-3:]
    w = SSD_CONV_DIM
    xn = xn_ref[...]
    conv = cb_ref[...] + cw_ref[SSD_CONV - 1:SSD_CONV, :] * xn
    for t in range(SSD_CONV - 1):
        conv = conv + cw_ref[t:t + 1, :] * cst_ref[:, t * w:(t + 1) * w]
    for t in range(SSD_CONV - 2):
        co_ref[:, t * w:(t + 1) * w] = cst_ref[:, (t + 1) * w:(t + 2) * w]
    co_ref[:, (SSD_CONV - 2) * w:(SSD_CONV - 1) * w] = xn
    xact = _silu(conv)
    xs = xact[:, :SSD_INNER]

    dt = _softplus(dt_ref[...] + dtb_ref[...])
    decay = jnp.exp(dt * (-jnp.exp(alog_ref[...])))
    dtx = xs * _expand_heads(dt, e_ref)
    et = et_ref[...]
    d1, d2, d3 = _split3(decay)
    dcol = _dot_nt(et, d1) + _dot_nt(et, d2) + _dot_nt(et, d3)

    eye = (lax.broadcasted_iota(jnp.int32, (LANE, LANE), 0) == lax.broadcasted_iota(jnp.int32, (LANE, LANE), 1))
    hg = SSD_HEADS // SSD_GROUPS
    pairs_per_group = hg * SSD_HEAD_DIM // LANE
    rows = []
    for bb in range(DEC_BB):
        pieces = []
        for t in range(SSD_INNER // LANE):
            g = t // pairs_per_group
            ts = slice(t * LANE, (t + 1) * LANE)
            bo = SSD_INNER + g * SSD_STATE
            co = SSD_INNER + (SSD_GROUPS + g) * SSD_STATE
            bmat = jnp.broadcast_to(xact[bb:bb + 1, bo:bo + SSD_STATE], (LANE, SSD_STATE)).astype(BF16)
            upd = _dot(_row_diag(dtx[bb:bb + 1, ts], eye), bmat)
            s_new = dcol[ts, bb:bb + 1] * s_ref[bb, ts, :] + upd
            so_ref[bb, ts, :] = s_new
            c8 = jnp.broadcast_to(xact[bb:bb + 1, co:co + SSD_STATE], (8, SSD_STATE)).astype(BF16)
            pieces.append(_dot_nt(c8, s_new.astype(BF16))[0:1, :])
        rows.append(jnp.concatenate(pieces, axis=1))
    y = jnp.concatenate(rows, axis=0) + dskip_ref[...] * xs
    y_ref[...] = _group_rms_gate(y, z_ref[...], nw_ref[...])


def _ssd_dec(u, dt, p, conv_all, s_all, l, prev):
    bsz = u.shape[0]
    _, e = _ssd_consts()
    et = e.T
    st = _state_spec(l, (SSD_INNER, SSD_STATE))
    cst = _state_spec(l, (CONV_ST_W,))
    in_specs = [pl.BlockSpec((DEC_BB, SSD_CONV_DIM), lambda i: (i, OFF_XBC // SSD_CONV_DIM)),
                pl.BlockSpec((DEC_BB, SSD_INNER), lambda i: (i, OFF_Z // SSD_INNER)),
                pl.BlockSpec((DEC_BB, LANE), lambda i: (i, 0)),
                cst, *_ssd_param_specs(l),
                _const_spec((LANE, SSD_INNER)), _const_spec((SSD_INNER, LANE)), st]
    alias_specs, alias_args, aliases = _alias_args(l, len(in_specs), prev)
    return pl.pallas_call(
        _ssd_dec_kernel,
        grid=(bsz // DEC_BB,),
        in_specs=in_specs + alias_specs,
        out_specs=[pl.BlockSpec((DEC_BB, SSD_INNER), lambda i: (i, 0)), cst, st],
        out_shape=[jax.ShapeDtypeStruct((bsz, SSD_INNER), F32),
                   jax.ShapeDtypeStruct(conv_all.shape, F32),
                   jax.ShapeDtypeStruct(s_all.shape, F32)],
        input_output_aliases=aliases,
        compiler_params=_params(("parallel",)),
        name="ssd_dec",
    )(u, u, dt, conv_all, *_ssd_param_args(p), e, et, s_all, *alias_args)


def _per_head_lanes(x, swap):
    lane = lax.broadcasted_iota(jnp.int32, x.shape, 1)
    first = lane < SWA_HEAD_DIM
    a = jnp.where(first, x, swap)
    b = jnp.where(first, swap, x)
    return jnp.concatenate([a, a, b, b], axis=1)


def _swa_dec_kernel(rel_ref, sink_ref, q_ref, kn_ref, vn_ref, idx_ref, kc_ref, vc_ref, *rest, layer):
    o_ref, ko_ref, vo_ref = rest[-3:]
    bias = _build_bias(rel_ref, idx_ref[...], (8, 2 * LANE))
    rowi = lax.broadcasted_iota(jnp.int32, (SWA_Q_HEADS, 1), 0)
    bias_c = jnp.zeros((SWA_Q_HEADS, WINDOW), F32)
    bias_n = jnp.zeros((SWA_Q_HEADS, 1), F32)
    sink = jnp.zeros((SWA_Q_HEADS, 1), F32)
    for h in range(SWA_Q_HEADS):
        bias_c = jnp.where(rowi == h, bias[h][0:1, :WINDOW], bias_c)
        bias_n = jnp.where(rowi == h, bias[h][0:1, WINDOW:WINDOW + 1], bias_n)
        sink = jnp.where(rowi == h, sink_ref[layer, h], sink)
    own = (lax.broadcasted_iota(jnp.int32, (SWA_Q_HEADS, SWA_QW), 1) // SWA_HEAD_DIM
           == lax.broadcasted_iota(jnp.int32, (SWA_Q_HEADS, SWA_QW), 0))
    scale = SWA_HEAD_DIM ** -0.5
    q = q_ref[...]
    kn = kn_ref[...]
    vn = vn_ref[...]
    knx = _per_head_lanes(kn, pltpu.roll(kn, SWA_HEAD_DIM, 1))
    vnx = _per_head_lanes(vn, pltpu.roll(vn, SWA_HEAD_DIM, 1))
    rows = []
    for bb in range(DEC_BB):
        kc = kc_ref[bb]
        vc = vc_ref[bb]
        ko_ref[bb, 0:WINDOW - 1, :] = kc[1:WINDOW, :]
        ko_ref[bb, WINDOW - 1:WINDOW, :] = kn[bb:bb + 1, :]
        vo_ref[bb, 0:WINDOW - 1, :] = vc[1:WINDOW, :]
        vo_ref[bb, WINDOW - 1:WINDOW, :] = vn[bb:bb + 1, :]
        qx = jnp.where(own, jnp.broadcast_to(q[bb:bb + 1, :], (SWA_Q_HEADS, SWA_QW)), 0.0)
        kx = _per_head_lanes(kc, pltpu.roll(kc, SWA_HEAD_DIM, 1)).astype(BF16)
        vx = _per_head_lanes(vc, pltpu.roll(vc, SWA_HEAD_DIM, 1)).astype(BF16)
        qxb = qx.astype(BF16)
        s_c = _dot_nt(qxb, kx) * scale + bias_c
        knb = knx[bb:bb + 1, :].astype(BF16).astype(F32)
        s_n = jnp.sum(qxb.astype(F32) * knb, axis=-1, keepdims=True) * scale + bias_n
        m = jnp.maximum(jnp.maximum(jnp.max(s_c, axis=-1, keepdims=True), s_n), sink)
        e_c = jnp.exp(s_c - m)
        e_n = jnp.exp(s_n - m)
        den = jnp.sum(e_c, axis=-1, keepdims=True) + e_n + jnp.exp(sink - m)
        pv = _dot((e_c / den).astype(BF16), vx) + (e_n / den) * vnx[bb:bb + 1, :]
        rows.append(jnp.sum(jnp.where(own, pv, 0.0), axis=0, keepdims=True))
    o_ref[...] = jnp.concatenate(rows, axis=0)


def _swa_dec(u, rel_bias, sinks, k_all, v_all, l, prev):
    bsz = u.shape[0]
    dist = np.zeros((8, 2 * LANE), np.int64)
    dist[:, :WINDOW + 1] = WINDOW - np.arange(WINDOW + 1)[None, :]
    idx = jnp.asarray(_t5_bucket_np(dist))
    smem = pl.BlockSpec(memory_space=pltpu.SMEM)
    cache = _state_spec(l, (WINDOW, SWA_KW))
    in_specs = [smem, smem,
                pl.BlockSpec((DEC_BB, SWA_QW), lambda i: (i, OFF_QC // SWA_QW)),
                pl.BlockSpec((DEC_BB, SWA_KW), lambda i: (i, OFF_KC // SWA_KW)),
                pl.BlockSpec((DEC_BB, SWA_KW), lambda i: (i, OFF_VC // SWA_KW)),
                _const_spec((8, 2 * LANE)), cache, cache]
    alias_specs, alias_args, aliases = _alias_args(l, len(in_specs), prev)
    return pl.pallas_call(
        functools.partial(_swa_dec_kernel, layer=l),
        grid=(bsz // DEC_BB,),
        in_specs=in_specs + alias_specs,
        out_specs=[pl.BlockSpec((DEC_BB, SWA_QW), lambda i: (i, 0)), cache, cache],
        out_shape=[jax.ShapeDtypeStruct((bsz, SWA_QW), F32),
                   jax.ShapeDtypeStruct(k_all.shape, F32), jax.ShapeDtypeStruct(v_all.shape, F32)],
        input_output_aliases=aliases,
        compiler_params=_params(("parallel",)),
        name="swa_dec",
    )(rel_bias, sinks, u, u, u, idx, k_all, v_all, *alias_args)


def _rope_tables(pos):
    half = RET_DK // 2
    inv = ROPE_BASE ** (-jnp.arange(half, dtype=F32) / half)
    ang = pos[:, None] * inv[None, :]
    cos = jnp.cos(ang)
    sin = jnp.sin(ang)
    return jnp.concatenate([cos, cos], -1), jnp.concatenate([-sin, sin], -1)


W_IN_SIZES = (RET_W, RET_W, RET_W, RET_W, SSD_INNER, SSD_CONV_DIM, SSD_HEADS, SWA_QW, SWA_KW, SWA_KW, 3 * D_MODEL)
W_IN_ORDER = (10, 0, 1, 2, 3, 4, 5, 7, 8, 9, 6)
W_PREP_ROWS = 256


def _w_in_prep_kernel(w_ref, o_ref):
    src = [sum(W_IN_SIZES[:i]) for i in range(len(W_IN_SIZES))]
    off = 0
    for idx in W_IN_ORDER:
        width = W_IN_SIZES[idx]
        o_ref[:, off:off + width] = w_ref[:, src[idx]:src[idx] + width].astype(BF16)
        off += width
    o_ref[:, off:] = jnp.zeros((o_ref.shape[0], N_PROJ - off), BF16)


def _reorder_w_in(w):
    depth, k, n_in = w.shape
    return pl.pallas_call(
        _w_in_prep_kernel,
        grid=(depth, k // W_PREP_ROWS),
        in_specs=[pl.BlockSpec((None, W_PREP_ROWS, n_in), lambda l, i: (l, i, 0))],
        out_specs=pl.BlockSpec((None, W_PREP_ROWS, N_PROJ), lambda l, i: (l, i, 0)),
        out_shape=jax.ShapeDtypeStruct((depth, k, N_PROJ), BF16),
        compiler_params=_params(("parallel", "parallel")),
        name="w_in_prep",
    )(w)


def _prepare_params(w_in, conv_w, conv_b, dt_bias, a_log, d_skip, ssd_norm_w, sinks,
                    w_br_ret, w_br_ssd, w_br_swa, w_out, ln1_g, ln1_b, ln2_g, ln2_b,
                    w_ffn_gate, w_ffn_up, w_ffn_down):
    row = lambda v: v.astype(F32)[:, None, :]
    pad = lambda v: jnp.pad(v.astype(F32), ((0, 0), (0, LANE - v.shape[1])))[:, None, :]
    return {
        "w_in": _reorder_w_in(w_in),
        "conv_w": conv_w.astype(F32), "conv_b": row(conv_b),
        "dt_bias": pad(dt_bias), "a_log": pad(a_log),
        "d_skip": row(jnp.repeat(d_skip, SSD_HEAD_DIM, axis=1)),
        "norm_w": row(ssd_norm_w), "sinks": sinks.astype(F32),
        "w_br_ret": w_br_ret.astype(BF16), "w_br_ssd": w_br_ssd.astype(BF16),
        "w_br_swa": w_br_swa.astype(BF16), "w_out": w_out.astype(BF16),
        "ln1_g": row(ln1_g), "ln1_b": row(ln1_b), "ln2_g": row(ln2_g), "ln2_b": row(ln2_b),
        "w_ffn_gate": w_ffn_gate.astype(BF16), "w_ffn_up": w_ffn_up.astype(BF16),
        "w_ffn_down": w_ffn_down.astype(BF16),
    }


def _prompt_layer(x, xb, p, l, rel_bias, cos, sin, bsz, seq):
    m = bsz * seq
    u, dt = _in_proj(xb, p["w_in"], l, min(512, m), BF16)
    u3 = u.reshape(bsz, seq, N_U)
    o_r, s_ret = _ret_prompt(u3, cos, sin)
    y, s_ssm = _ssd_prompt(u3, dt.reshape(bsz, seq, LANE), p, l)
    o_c = _swa_prompt(u3, rel_bias, p["sinks"], l)
    x1, x1b = _merge(o_r.reshape(m, RET_W), y.reshape(m, SSD_INNER), o_c.reshape(m, SWA_QW), u, x, p, l,
                     min(512, m))
    x2, x2b = _ffn(x1b, x1, p, l, min(512, m))
    conv = u3[:, seq - (SSD_CONV - 1):, OFF_XBC:OFF_XBC + SSD_CONV_DIM].astype(F32)
    k_new = u3[:, seq - WINDOW:, OFF_KC:OFF_KC + SWA_KW].astype(F32)
    v_new = u3[:, seq - WINDOW:, OFF_VC:OFF_VC + SWA_KW].astype(F32)
    return x2, x2b, (s_ret, s_ssm.reshape(bsz, SSD_HEADS, SSD_HEAD_DIM, SSD_STATE), conv,
                     k_new.reshape(bsz, WINDOW, SWA_KV_HEADS, SWA_HEAD_DIM),
                     v_new.reshape(bsz, WINDOW, SWA_KV_HEADS, SWA_HEAD_DIM))


def _sample_layer(x, xb, p, l, rel_bias, cos, sin, states, prev):
    s_ret, s_ssm, s_conv, kbuf, vbuf = states
    bsz = x.shape[0]
    u, dt = _in_proj(xb, p["w_in"], l, bsz, F32)
    pr = prev if l else (None,) * 5
    o_r, ret_new = _ret_dec(u, cos, sin, s_ret, l, pr[0:1])
    y, conv_new, ssm_new = _ssd_dec(u, dt, p, s_conv, s_ssm, l, pr[1:3])
    o_c, k_new, v_new = _swa_dec(u, rel_bias, p["sinks"], kbuf, vbuf, l, pr[3:5])
    x1, x1b = _merge(o_r.astype(BF16), y.astype(BF16), o_c.astype(BF16), u, x, p, l, bsz)
    x2, x2b = _ffn(x1b, x1, p, l, bsz)
    return x2, x2b, (ret_new, conv_new, ssm_new, k_new, v_new)


def kernel(x_prompt, x_sample, state_ret, state_ssm, state_conv, cache_swa_k, cache_swa_v, w_in, conv_w, conv_b, dt_bias, a_log, d_skip, ssd_norm_w, sinks, rel_bias, w_br_ret, w_br_ssd, w_br_swa, w_out, ln1_g, ln1_b, ln2_g, ln2_b, w_ffn_gate, w_ffn_up, w_ffn_down):
    bsz, seq, _ = x_prompt.shape
    dbsz, dseq, _ = x_sample.shape
    assert seq % CHUNK == 0 and dseq == 1 and dbsz % DEC_BB == 0
    p = _prepare_params(w_in, conv_w, conv_b, dt_bias, a_log, d_skip, ssd_norm_w, sinks,
                        w_br_ret, w_br_ssd, w_br_swa, w_out, ln1_g, ln1_b, ln2_g, ln2_b,
                        w_ffn_gate, w_ffn_up, w_ffn_down)
    cos_p, sin_p = _rope_tables(jnp.arange(seq, dtype=F32))
    cos_s, sin_s = _rope_tables(PAST_LEN + jnp.arange(dseq, dtype=F32))
    rel = rel_bias.astype(F32)
    states = (state_ret.astype(F32),
              state_ssm.astype(F32).reshape(DEPTH, dbsz, SSD_INNER, SSD_STATE),
              state_conv.astype(F32).reshape(DEPTH, dbsz, CONV_ST_W),
              cache_swa_k.astype(F32).reshape(DEPTH, dbsz, WINDOW, SWA_KW),
              cache_swa_v.astype(F32).reshape(DEPTH, dbsz, WINDOW, SWA_KW))

    xp = x_prompt.reshape(bsz * seq, D_MODEL).astype(F32)
    xs = x_sample.reshape(dbsz * dseq, D_MODEL).astype(F32)
    xpb, xsb = xp.astype(BF16), xs.astype(BF16)
    new_p = [[] for _ in range(5)]
    new_s = None
    for l in range(DEPTH):
        xp, xpb, st_p = _prompt_layer(xp, xpb, p, l, rel, cos_p, sin_p, bsz, seq)
        xs, xsb, (ret_new, conv_new, ssm_new, k_new, v_new) = _sample_layer(
            xs, xsb, p, l, rel, cos_s, sin_s, states, new_s)
        new_s = (ret_new, conv_new, ssm_new, k_new, v_new)
        for j in range(5):
            new_p[j].append(st_p[j])
    st_p = [jnp.stack(s) for s in new_p]
    ret_s, conv_s, ssm_s, k_s, v_s = new_s
    return (xp.reshape(bsz, seq, D_MODEL), xs.reshape(dbsz, dseq, D_MODEL), *st_p,
            ret_s, ssm_s.reshape(state_ssm.shape), conv_s.reshape(state_conv.shape),
            k_s.reshape(cache_swa_k.shape), v_s.reshape(cache_swa_v.shape))
```

```python
import functools
import math

import jax
import jax.numpy as jnp
import numpy as np
from jax import lax
from jax.experimental import pallas as pl
from jax.experimental.pallas import tpu as pltpu

F32 = jnp.float32
BF16 = jnp.bfloat16

D_MODEL = 1024
DEPTH = 4
PAST_LEN = 8192
RET_HEADS = 4
RET_DK = 128
RET_DV = 128
ROPE_BASE = 10000.0
SSD_HEADS = 16
SSD_HEAD_DIM = 64
SSD_GROUPS = 2
SSD_STATE = 128
SSD_CONV = 4
SSD_INNER = SSD_HEADS * SSD_HEAD_DIM
SSD_CONV_DIM = SSD_INNER + 2 * SSD_GROUPS * SSD_STATE
SWA_Q_HEADS = 8
SWA_KV_HEADS = 2
SWA_HEAD_DIM = 64
WINDOW = 128
N_BUCKETS = 32
BUCKET_MAX_DIST = 128
D_FF = ((8 * D_MODEL // 3 + 255) // 256) * 256
ALPHA = (2 * DEPTH) ** 0.25
EPS = 1e-5
CHUNK = 128

RET_W = RET_HEADS * RET_DK
SWA_QW = SWA_Q_HEADS * SWA_HEAD_DIM
SWA_KW = SWA_KV_HEADS * SWA_HEAD_DIM
CONV_ST_W = (SSD_CONV - 1) * SSD_CONV_DIM

OFF_GATES = 0
OFF_QR = 3 * D_MODEL
OFF_KR = OFF_QR + RET_W
OFF_VR = OFF_KR + RET_W
OFF_GR = OFF_VR + RET_W
OFF_Z = OFF_GR + RET_W
OFF_XBC = OFF_Z + SSD_INNER
OFF_QC = OFF_XBC + SSD_CONV_DIM
OFF_KC = OFF_QC + SWA_QW
OFF_VC = OFF_KC + SWA_KW
N_U = OFF_VC + SWA_KW
LANE = 128
N_PROJ = N_U + LANE
PROJ_CHUNK = 1024

VMEM_LIMIT = 48 * 1024 * 1024
NEG_BIG = -1e30


def _sigmoid(x):
    return 0.5 * jnp.tanh(0.5 * x) + 0.5


def _silu(x):
    h = 0.5 * x
    return h * jnp.tanh(h) + h


def _softplus(x):
    return jnp.maximum(x, 0.0) + jnp.log1p(jnp.exp(-jnp.abs(x)))


def _dot(a, b):
    return jnp.dot(a, b, preferred_element_type=F32)


def _dot_nt(a, b):
    return lax.dot_general(a, b, (((1,), (1,)), ((), ())), preferred_element_type=F32)


def _dot_tn(a, b):
    return lax.dot_general(a, b, (((0,), (0,)), ((), ())), preferred_element_type=F32)


def _split2(x):
    hi = x.astype(BF16)
    lo = (x - hi.astype(F32)).astype(BF16)
    return hi, lo


def _split3(x):
    hi = x.astype(BF16)
    r = x - hi.astype(F32)
    mid = r.astype(BF16)
    lo = (r - mid.astype(F32)).astype(BF16)
    return hi, mid, lo


def _params(sem):
    return pltpu.CompilerParams(dimension_semantics=sem, vmem_limit_bytes=VMEM_LIMIT)


def _layer_spec(l, shape, single_buffer=False):
    nd = len(shape)
    mode = pl.Buffered(1) if single_buffer else None
    return pl.BlockSpec((None,) + tuple(shape), lambda *_: (l,) + (0,) * nd, pipeline_mode=mode)


def _const_spec(shape):
    nd = len(shape)
    return pl.BlockSpec(tuple(shape), lambda *_: (0,) * nd)


_ANY = pl.BlockSpec(memory_space=pl.ANY)


def _in_proj_kernel(x_ref, w_ref, u_ref, dt_ref):
    x = x_ref[...].astype(BF16)
    for a in range(0, N_U, PROJ_CHUNK):
        b = min(a + PROJ_CHUNK, N_U)
        u_ref[:, a:b] = _dot(x, w_ref[:, a:b]).astype(u_ref.dtype)
    dt_ref[...] = _dot(x, w_ref[:, N_U:N_PROJ])


def _in_proj(xb, w_all, l, tm, u_dtype):
    m = xb.shape[0]
    return pl.pallas_call(
        _in_proj_kernel,
        grid=(m // tm,),
        in_specs=[pl.BlockSpec((tm, D_MODEL), lambda i: (i, 0)),
                  _layer_spec(l, (D_MODEL, N_PROJ), single_buffer=True)],
        out_specs=[pl.BlockSpec((tm, N_U), lambda i: (i, 0)), pl.BlockSpec((tm, LANE), lambda i: (i, 0))],
        out_shape=[jax.ShapeDtypeStruct((m, N_U), u_dtype), jax.ShapeDtypeStruct((m, LANE), F32)],
        compiler_params=_params(("parallel",)),
        name="in_proj",
    )(xb, w_all)


def _rotary(x, cos, sin):
    return x * cos + pltpu.roll(x, RET_DK // 2, 1) * sin


def _seqs_per_step(bsz):
    return next(n for n in (4, 2, 1) if bsz % n == 0)


def _u_spec(nb, off, width):
    return pl.BlockSpec((nb, CHUNK, width), lambda b, c: (b, c, off // width))


def _head_norm(o):
    mu = jnp.mean(o, axis=-1, keepdims=True)
    d = o - mu
    var = jnp.mean(d * d, axis=-1, keepdims=True)
    return d * lax.rsqrt(var + EPS)


def _head_norm_gate(o, g):
    return _head_norm(o) * _silu(g)


def _ret_prompt_kernel(q_ref, k_ref, v_ref, g_ref, cos_ref, sin_ref, dmat_ref, kdec_ref, qdec_ref,
                       o_ref, sfin_ref, s_scr, *, cdec):
    c = pl.program_id(1)

    @pl.when(c == 0)
    def _():
        s_scr[...] = jnp.zeros_like(s_scr)

    cos = cos_ref[...]
    sin = sin_ref[...]
    for bi in range(q_ref.shape[0]):
        for h in range(RET_HEADS):
            sl = slice(h * RET_DK, (h + 1) * RET_DK)
            qr = _rotary(q_ref[bi, :, sl].astype(F32), cos, sin)
            kr = _rotary(k_ref[bi, :, sl].astype(F32), cos, sin) * (RET_DK ** -0.5)
            qb = qr.astype(BF16)
            vb = v_ref[bi, :, sl]
            sc = _dot_nt(qb, kr.astype(BF16)) * dmat_ref[h]
            s_prev = s_scr[bi, h]
            o = _dot(sc.astype(BF16), vb) + _dot(qb, s_prev.astype(BF16)) * qdec_ref[:, sl]
            kd = (kr * kdec_ref[:, sl]).astype(BF16)
            s_scr[bi, h] = cdec[h] * s_prev + _dot_tn(kd, vb)
            o_ref[bi, :, sl] = _head_norm_gate(o, g_ref[bi, :, sl].astype(F32)).astype(o_ref.dtype)

    @pl.when(c == pl.num_programs(1) - 1)
    def _():
        sfin_ref[...] = s_scr[...]


def _ret_consts(c):
    lg = [math.log(1.0 - 2.0 ** (-5.0 - h)) for h in range(RET_HEADS)]
    i = np.arange(c, dtype=np.float64)
    rel = i[:, None] - i[None, :]
    dmat = np.stack([np.where(rel >= 0, np.exp(l * np.maximum(rel, 0.0)), 0.0) for l in lg])
    kdec = np.concatenate([np.repeat(np.exp(l * (c - 1 - i))[:, None], RET_DK, 1) for l in lg], 1)
    qdec = np.concatenate([np.repeat(np.exp(l * (i + 1.0))[:, None], RET_DK, 1) for l in lg], 1)
    cdec = tuple(math.exp(l * c) for l in lg)
    return (jnp.asarray(dmat, F32), jnp.asarray(kdec, F32), jnp.asarray(qdec, F32), cdec)


def _ret_prompt(u3, cos, sin):
    bsz, seq, _ = u3.shape
    nb = _seqs_per_step(bsz)
    dmat, kdec, qdec, cdec = _ret_consts(CHUNK)
    rot = pl.BlockSpec((CHUNK, RET_DK), lambda b, c: (c, 0))
    return pl.pallas_call(
        functools.partial(_ret_prompt_kernel, cdec=cdec),
        grid=(bsz // nb, seq // CHUNK),
        in_specs=[_u_spec(nb, OFF_QR, RET_W), _u_spec(nb, OFF_KR, RET_W), _u_spec(nb, OFF_VR, RET_W),
                  _u_spec(nb, OFF_GR, RET_W), rot, rot,
                  _const_spec((RET_HEADS, CHUNK, CHUNK)), _const_spec((CHUNK, RET_W)),
                  _const_spec((CHUNK, RET_W))],
        out_specs=[pl.BlockSpec((nb, CHUNK, RET_W), lambda b, c: (b, c, 0)),
                   pl.BlockSpec((nb, RET_HEADS, RET_DK, RET_DV), lambda b, c: (b, 0, 0, 0))],
        out_shape=[jax.ShapeDtypeStruct((bsz, seq, RET_W), BF16),
                   jax.ShapeDtypeStruct((bsz, RET_HEADS, RET_DK, RET_DV), F32)],
        scratch_shapes=[pltpu.VMEM((nb, RET_HEADS, RET_DK, RET_DV), F32)],
        compiler_params=_params(("arbitrary", "arbitrary")),
        name="ret_prompt",
    )(u3, u3, u3, u3, cos, sin, dmat, kdec, qdec)


def _expand_heads(v, e_ref):
    hi, lo = _split2(v)
    e = e_ref[...]
    return _dot(hi, e) + _dot(lo, e)


def _group_rms_gate(y, z, w):
    return _group_rms(y * _silu(z), w)


def _group_rms(y, w):
    half = SSD_INNER // SSD_GROUPS
    outs = []
    for g in range(SSD_GROUPS):
        yg = y[:, g * half:(g + 1) * half]
        ms = jnp.mean(yg * yg, axis=-1, keepdims=True)
        outs.append(yg * lax.rsqrt(ms + EPS))
    return jnp.concatenate(outs, axis=-1) * w


CONV_TAIL = 16


def _ssd_prompt_kernel(xbc_ref, z_ref, dt_ref, cw_ref, cb_ref, dtb_ref, alog_ref, dskip_ref, nw_ref,
                       tril_ref, e_ref, shift_ref, shift_tail_ref, y_ref, sfin_ref, tail_scr, st_scr, yi_scr):
    c = pl.program_id(1)

    @pl.when(c == 0)
    def _():
        tail_scr[...] = jnp.zeros_like(tail_scr)
        st_scr[...] = jnp.zeros_like(st_scr)

    taps = [cw_ref[t:t + 1, :].astype(BF16) for t in range(SSD_CONV)]
    a = -jnp.exp(alog_ref[...])
    tril = tril_ref[...]
    ri = lax.broadcasted_iota(jnp.int32, (CHUNK, CHUNK), 0)
    ci = lax.broadcasted_iota(jnp.int32, (CHUNK, CHUNK), 1)
    causal = ri >= ci
    first_head = ci < SSD_HEAD_DIM
    second_head = ci >= SSD_HEAD_DIM
    hg = SSD_HEADS // SSD_GROUPS
    gw = hg * SSD_HEAD_DIM
    for bi in range(xbc_ref.shape[0]):
        xbc = xbc_ref[bi]
        tail = tail_scr[bi]
        prod = jnp.concatenate([xbc * w for w in taps], axis=0)
        prod_tail = jnp.concatenate([tail * w for w in taps[:SSD_CONV - 1]], axis=0)
        conv = _dot(shift_ref[...], prod) + _dot(shift_tail_ref[...], prod_tail) + cb_ref[...]
        tail_scr[bi] = xbc[CHUNK - CONV_TAIL:, :]
        xact = _silu(conv)
        xs = xact[:, :SSD_INNER]
        xs_b = xs.astype(BF16)

        dt = _softplus(dt_ref[bi] + dtb_ref[...])
        d1, d2, d3 = _split3(dt * a)
        cs = _dot(tril, d1) + _dot(tril, d2) + _dot(tril, d3)
        cs_last = cs[CHUNK - 1:CHUNK, :]
        cs_t = cs.T
        dt_t = dt.T
        exp_cs = _expand_heads(jnp.exp(cs), e_ref)
        xw = (xs * _expand_heads(jnp.exp(cs_last - cs) * dt, e_ref)).astype(BF16)

        for g in range(SSD_GROUPS):
            bg = xact[:, SSD_INNER + g * SSD_STATE:SSD_INNER + (g + 1) * SSD_STATE].astype(BF16)
            cg = xact[:, SSD_INNER + (SSD_GROUPS + g) * SSD_STATE:
                      SSD_INNER + (SSD_GROUPS + g + 1) * SSD_STATE].astype(BF16)
            cb = _dot_nt(cg, bg)
            for hh in range(0, hg, 2):
                acc = None
                ps = slice((g * hg + hh) * SSD_HEAD_DIM, (g * hg + hh + 2) * SSD_HEAD_DIM)
                xp = xs_b[:, ps]
                for k in range(2):
                    h = g * hg + hh + k
                    seg = cs[:, h:h + 1] - cs_t[h:h + 1, :]
                    lmat = jnp.exp(jnp.where(causal, seg, NEG_BIG))
                    w = (cb * lmat * dt_t[h:h + 1, :]).astype(BF16)
                    part = _dot(w, jnp.where(first_head if k == 0 else second_head, xp, jnp.zeros_like(xp)))
                    acc = part if acc is None else acc + part
                yi_scr[bi, :, ps] = acc
            gs = slice(g * gw, (g + 1) * gw)
            st_prev = st_scr[bi, :, gs]
            yi_scr[bi, :, gs] = yi_scr[bi, :, gs] + _dot(cg, st_prev.astype(BF16)) * exp_cs[:, gs]
            st_scr[bi, :, gs] = st_prev * exp_cs[CHUNK - 1:CHUNK, gs] + _dot_tn(bg, xw[:, gs])

        y = yi_scr[bi] + dskip_ref[...] * xs
        y_ref[bi] = _group_rms_gate(y, z_ref[bi].astype(F32), nw_ref[...]).astype(y_ref.dtype)

    @pl.when(c == pl.num_programs(1) - 1)
    def _():
        for bi in range(xbc_ref.shape[0]):
            for t in range(SSD_INNER // LANE):
                sfin_ref[bi, t * LANE:(t + 1) * LANE, :] = st_scr[bi, :, t * LANE:(t + 1) * LANE].T


def _ssd_consts():
    tril = np.tril(np.ones((CHUNK, CHUNK), np.float32))
    e = np.zeros((LANE, SSD_INNER), np.float32)
    for h in range(SSD_HEADS):
        e[h, h * SSD_HEAD_DIM:(h + 1) * SSD_HEAD_DIM] = 1.0
    return jnp.asarray(tril, BF16), jnp.asarray(e, BF16)


def _ssd_param_specs(l):
    return [_layer_spec(l, (SSD_CONV, SSD_CONV_DIM)), _layer_spec(l, (1, SSD_CONV_DIM)),
            _layer_spec(l, (1, LANE)), _layer_spec(l, (1, LANE)),
            _layer_spec(l, (1, SSD_INNER)), _layer_spec(l, (1, SSD_INNER))]


def _ssd_param_args(p):
    return (p["conv_w"], p["conv_b"], p["dt_bias"], p["a_log"], p["d_skip"], p["norm_w"])


def _conv_shift_consts():
    k = SSD_CONV
    shift = np.zeros((CHUNK, k * CHUNK), np.float32)
    shift_tail = np.zeros((CHUNK, (k - 1) * CONV_TAIL), np.float32)
    for t in range(k):
        for i in range(CHUNK):
            j = i - (k - 1) + t
            if j >= 0:
                shift[i, t * CHUNK + j] = 1.0
            else:
                shift_tail[i, t * CONV_TAIL + CONV_TAIL + j] = 1.0
    return jnp.asarray(shift, BF16), jnp.asarray(shift_tail, BF16)


def _ssd_prompt(u3, dt3, p, l):
    bsz, seq, _ = u3.shape
    nb = _seqs_per_step(bsz)
    tril, e = _ssd_consts()
    shift, shift_tail = _conv_shift_consts()
    return pl.pallas_call(
        _ssd_prompt_kernel,
        grid=(bsz // nb, seq // CHUNK),
        in_specs=[_u_spec(nb, OFF_XBC, SSD_CONV_DIM), _u_spec(nb, OFF_Z, SSD_INNER),
                  pl.BlockSpec((nb, CHUNK, LANE), lambda b, c: (b, c, 0)),
                  *_ssd_param_specs(l),
                  _const_spec(tril.shape), _const_spec(e.shape), _const_spec(shift.shape),
                  _const_spec(shift_tail.shape)],
        out_specs=[pl.BlockSpec((nb, CHUNK, SSD_INNER), lambda b, c: (b, c, 0)),
                   pl.BlockSpec((nb, SSD_INNER, SSD_STATE), lambda b, c: (b, 0, 0))],
        out_shape=[jax.ShapeDtypeStruct((bsz, seq, SSD_INNER), BF16),
                   jax.ShapeDtypeStruct((bsz, SSD_INNER, SSD_STATE), F32)],
        scratch_shapes=[pltpu.VMEM((nb, CONV_TAIL, SSD_CONV_DIM), BF16),
                        pltpu.VMEM((nb, SSD_STATE, SSD_INNER), F32),
                        pltpu.VMEM((nb, CHUNK, SSD_INNER), F32)],
        compiler_params=_params(("arbitrary", "arbitrary")),
        name="ssd_prompt",
    )(u3, u3, dt3, *_ssd_param_args(p), tril, e, shift, shift_tail)


def _t5_bucket_np(dist):
    max_exact = N_BUCKETS // 2
    df = np.maximum(dist, 1).astype(np.float32)
    large = max_exact + (np.log(df / np.float32(max_exact)) / np.float32(math.log(BUCKET_MAX_DIST / max_exact))
                         * np.float32(N_BUCKETS - max_exact)).astype(np.int32)
    large = np.minimum(large, N_BUCKETS - 1)
    return np.where(dist < max_exact, dist, large).astype(np.int32)


def _build_bias(rel_ref, idx, shape):
    out = []
    for h in range(SWA_Q_HEADS):
        acc = jnp.zeros(shape, F32)
        for b in range(N_BUCKETS):
            acc = jnp.where(idx == b, rel_ref[b, h], acc)
        out.append(acc)
    return out


def _swa_prompt_kernel(rel_ref, sink_ref, q_ref, kp_ref, kc_ref, vp_ref, vc_ref, idx_ref, o_ref, bias_scr,
                       *, layer):
    b = pl.program_id(0)
    n = pl.program_id(1)
    grp = SWA_Q_HEADS // SWA_KV_HEADS

    @pl.when(jnp.logical_and(b == 0, n == 0))
    def _():
        qi = lax.broadcasted_iota(jnp.int32, (CHUNK, 2 * CHUNK), 0)
        kj = lax.broadcasted_iota(jnp.int32, (CHUNK, 2 * CHUNK), 1)
        dist = qi + WINDOW - kj
        band = jnp.logical_and(dist >= 0, dist <= WINDOW)
        for h, bias in enumerate(_build_bias(rel_ref, idx_ref[...], (CHUNK, 2 * CHUNK))):
            g, hh = divmod(h, grp)
            rows = slice(hh * CHUNK, (hh + 1) * CHUNK)
            bias_scr[1, g, rows, :] = jnp.where(band, bias, NEG_BIG)
            bias_scr[0, g, rows, :] = jnp.where(jnp.logical_and(band, kj >= WINDOW), bias, NEG_BIG)

    tbl = jnp.minimum(n, 1)
    gq = grp * SWA_HEAD_DIM
    lane_head = lax.broadcasted_iota(jnp.int32, (CHUNK, gq), 1) // SWA_HEAD_DIM
    for bi in range(q_ref.shape[0]):
        kk = jnp.concatenate([kp_ref[bi], kc_ref[bi]], axis=0)
        vv = jnp.concatenate([vp_ref[bi], vc_ref[bi]], axis=0)
        for g in range(SWA_KV_HEADS):
            ks = slice(g * SWA_HEAD_DIM, (g + 1) * SWA_HEAD_DIM)
            k_rep = jnp.concatenate([kk[:, ks]] * grp, axis=1)
            v_rep = jnp.concatenate([vv[:, ks]] * grp, axis=1)
            qg = q_ref[bi, :, g * gq:(g + 1) * gq] * (SWA_HEAD_DIM ** -0.5)
            q_stack = jnp.concatenate(
                [jnp.where(lane_head == hh, qg, jnp.zeros_like(qg)) for hh in range(grp)], axis=0)
            logits = _dot_nt(q_stack, k_rep) + bias_scr[tbl, g]
            es, rden = [], []
            for hh in range(grp):
                lg = logits[hh * CHUNK:(hh + 1) * CHUNK, :]
                sink = sink_ref[layer, g * grp + hh]
                m = jnp.maximum(jnp.max(lg, axis=-1, keepdims=True), sink)
                e = jnp.exp(lg - m)
                rden.append(1.0 / (jnp.sum(e, axis=-1, keepdims=True) + jnp.exp(sink - m)))
                es.append(e.astype(BF16))
            pv = _dot(jnp.concatenate(es, axis=0), v_rep)
            o = jnp.zeros((CHUNK, gq), F32)
            for hh in range(grp):
                o = jnp.where(lane_head == hh, pv[hh * CHUNK:(hh + 1) * CHUNK, :] * rden[hh], o)
            o_ref[bi, :, g * gq:(g + 1) * gq] = o.astype(o_ref.dtype)


def _swa_prompt(u3, rel_bias, sinks, l):
    bsz, seq, _ = u3.shape
    nb = _seqs_per_step(bsz)
    qi = np.arange(CHUNK)[:, None]
    kj = np.arange(2 * CHUNK)[None, :]
    idx = jnp.asarray(_t5_bucket_np(np.maximum(qi + WINDOW - kj, 0)))
    cur = lambda off, w: pl.BlockSpec((nb, CHUNK, w), lambda b, n: (b, n, off // w))
    prev = lambda off, w: pl.BlockSpec((nb, CHUNK, w), lambda b, n: (b, jnp.maximum(n - 1, 0), off // w))
    smem = pl.BlockSpec(memory_space=pltpu.SMEM)
    grp = SWA_Q_HEADS // SWA_KV_HEADS
    return pl.pallas_call(
        functools.partial(_swa_prompt_kernel, layer=l),
        grid=(bsz // nb, seq // CHUNK),
        in_specs=[smem, smem, cur(OFF_QC, SWA_QW),
                  prev(OFF_KC, SWA_KW), cur(OFF_KC, SWA_KW), prev(OFF_VC, SWA_KW), cur(OFF_VC, SWA_KW),
                  _const_spec((CHUNK, 2 * CHUNK))],
        out_specs=pl.BlockSpec((nb, CHUNK, SWA_QW), lambda b, n: (b, n, 0)),
        out_shape=jax.ShapeDtypeStruct((bsz, seq, SWA_QW), BF16),
        scratch_shapes=[pltpu.VMEM((2, SWA_KV_HEADS, grp * CHUNK, 2 * CHUNK), F32)],
        compiler_params=_params(("arbitrary", "arbitrary")),
        name="swa_prompt",
    )(rel_bias, sinks, u3, u3, u3, u3, u3, idx)


def _layer_norm(r, g, b):
    mu = jnp.mean(r, axis=-1, keepdims=True)
    d = r - mu
    var = jnp.mean(d * d, axis=-1, keepdims=True)
    return d * lax.rsqrt(var + EPS) * g + b


def _merge_kernel(or_ref, y_ref, oc_ref, ga_ref, gb_ref, gc_ref, x_ref, wa_ref, wb_ref, wc_ref, wo_ref,
                  g_ref, b_ref, xo_ref, xob_ref):
    mix = (_sigmoid(ga_ref[...].astype(F32)) * _dot(or_ref[...], wa_ref[...])
           + _sigmoid(gb_ref[...].astype(F32)) * _dot(y_ref[...], wb_ref[...])
           + _sigmoid(gc_ref[...].astype(F32)) * _dot(oc_ref[...], wc_ref[...]))
    r = ALPHA * x_ref[...] + _dot(mix.astype(BF16), wo_ref[...])
    xn = _layer_norm(r, g_ref[...], b_ref[...])
    xo_ref[...] = xn
    xob_ref[...] = xn.astype(BF16)


def _merge(o_r, y, o_c, u, x, p, l, tm):
    m = x.shape[0]
    rows = lambda w: pl.BlockSpec((tm, w), lambda i: (i, 0))
    gate = lambda k: pl.BlockSpec((tm, D_MODEL), lambda i: (i, OFF_GATES // D_MODEL + k))
    wspec = lambda k: _layer_spec(l, (k, D_MODEL), single_buffer=True)
    return pl.pallas_call(
        _merge_kernel,
        grid=(m // tm,),
        in_specs=[rows(RET_W), rows(SSD_INNER), rows(SWA_QW), gate(0), gate(1), gate(2), rows(D_MODEL),
                  wspec(RET_W), wspec(SSD_INNER), wspec(SWA_QW), wspec(D_MODEL),
                  _layer_spec(l, (1, D_MODEL)), _layer_spec(l, (1, D_MODEL))],
        out_specs=[rows(D_MODEL), rows(D_MODEL)],
        out_shape=[jax.ShapeDtypeStruct((m, D_MODEL), F32), jax.ShapeDtypeStruct((m, D_MODEL), BF16)],
        compiler_params=_params(("parallel",)),
        name="merge",
    )(o_r, y, o_c, u, u, u, x, p["w_br_ret"], p["w_br_ssd"], p["w_br_swa"], p["w_out"],
      p["ln1_g"], p["ln1_b"])


def _ffn_kernel(xb_ref, x_ref, wg_ref, wu_ref, wd_ref, g_ref, b_ref, xo_ref, xob_ref):
    xb = xb_ref[...]
    hmid = _silu(_dot(xb, wg_ref[...])) * _dot(xb, wu_ref[...])
    r = ALPHA * x_ref[...] + _dot(hmid.astype(BF16), wd_ref[...])
    xn = _layer_norm(r, g_ref[...], b_ref[...])
    xo_ref[...] = xn
    xob_ref[...] = xn.astype(BF16)


def _ffn(xb, x, p, l, tm):
    m = x.shape[0]
    rows = pl.BlockSpec((tm, D_MODEL), lambda i: (i, 0))
    return pl.pallas_call(
        _ffn_kernel,
        grid=(m // tm,),
        in_specs=[rows, rows,
                  _layer_spec(l, (D_MODEL, D_FF), single_buffer=True),
                  _layer_spec(l, (D_MODEL, D_FF), single_buffer=True),
                  _layer_spec(l, (D_FF, D_MODEL), single_buffer=True),
                  _layer_spec(l, (1, D_MODEL)), _layer_spec(l, (1, D_MODEL))],
        out_specs=[rows, rows],
        out_shape=[jax.ShapeDtypeStruct((m, D_MODEL), F32), jax.ShapeDtypeStruct((m, D_MODEL), BF16)],
        compiler_params=_params(("parallel",)),
        name="ffn",
    )(xb, x, p["w_ffn_gate"], p["w_ffn_up"], p["w_ffn_down"], p["ln2_g"], p["ln2_b"])


DEC_BB = 8


def _state_spec(l, shape):
    nd = len(shape)
    return pl.BlockSpec((None, DEC_BB) + tuple(shape), lambda i: (l, i) + (0,) * nd)


def _alias_args(l, n_in, prev):
    if l == 0:
        return [], [], {}
    return [_ANY] * len(prev), list(prev), {n_in + k: 1 + k for k in range(len(prev))}


def _row_diag(row, eye):
    n = row.shape[1]
    return jnp.where(eye, jnp.broadcast_to(row, (n, n)), 0.0).astype(BF16)


def _ret_dec_kernel(q_ref, k_ref, v_ref, g_ref, cos_ref, sin_ref, s_ref, *rest, gamma):
    o_ref, so_ref = rest[-2:]
    eye = (lax.broadcasted_iota(jnp.int32, (RET_DK, RET_DK), 0)
           == lax.broadcasted_iota(jnp.int32, (RET_DK, RET_DK), 1))
    cos = cos_ref[...]
    sin = sin_ref[...]
    for h in range(RET_HEADS):
        sl = slice(h * RET_DK, (h + 1) * RET_DK)
        qr = _rotary(q_ref[:, sl], cos, sin)
        kr = _rotary(k_ref[:, sl], cos, sin) * (RET_DK ** -0.5)
        v = v_ref[:, sl]
        rows = []
        for bb in range(DEC_BB):
            kv = _dot(_row_diag(kr[bb:bb + 1, :], eye),
                      jnp.broadcast_to(v[bb:bb + 1, :], (RET_DK, RET_DV)).astype(BF16))
            s_new = gamma[h] * s_ref[bb, h] + kv
            so_ref[bb, h] = s_new
            q8 = jnp.broadcast_to(qr[bb:bb + 1, :], (8, RET_DK)).astype(BF16)
            rows.append(_dot(q8, s_new.astype(BF16))[0:1, :])
        o = jnp.concatenate(rows, axis=0)
        o_ref[:, sl] = _head_norm_gate(o, g_ref[:, sl])


def _ret_dec(u, cos, sin, s_all, l, prev):
    bsz = u.shape[0]
    gamma = tuple(1.0 - 2.0 ** (-5.0 - h) for h in range(RET_HEADS))
    usec = lambda off: pl.BlockSpec((DEC_BB, RET_W), lambda i: (i, off // RET_W))
    st = _state_spec(l, (RET_HEADS, RET_DK, RET_DV))
    rot = _const_spec((1, RET_DK))
    in_specs = [usec(OFF_QR), usec(OFF_KR), usec(OFF_VR), usec(OFF_GR), rot, rot, st]
    alias_specs, alias_args, aliases = _alias_args(l, len(in_specs), prev)
    return pl.pallas_call(
        functools.partial(_ret_dec_kernel, gamma=gamma),
        grid=(bsz // DEC_BB,),
        in_specs=in_specs + alias_specs,
        out_specs=[pl.BlockSpec((DEC_BB, RET_W), lambda i: (i, 0)), st],
        out_shape=[jax.ShapeDtypeStruct((bsz, RET_W), F32), jax.ShapeDtypeStruct(s_all.shape, F32)],
        input_output_aliases=aliases,
        compiler_params=_params(("parallel",)),
        name="ret_dec",
    )(u, u, u, u, cos, sin, s_all, *alias_args)


def _ssd_dec_kernel(xn_ref, z_ref, dt_ref, cst_ref, cw_ref, cb_ref, dtb_ref, alog_ref, dskip_ref, nw_ref,
                    e_ref, et_ref, s_ref, *rest):
    y_ref, co_ref, so_ref = rest[-3:]
    w = SSD_CONV_DIM
    xn = xn_ref[...]
    conv = cb_ref[...] + cw_ref[SSD_CONV - 1:SSD_CONV, :] * xn
    for t in range(SSD_CONV - 1):
        conv = conv + cw_ref[t:t + 1, :] * cst_ref[:, t * w:(t + 1) * w]
    for t in range(SSD_CONV - 2):
        co_ref[:, t * w:(t + 1) * w] = cst_ref[:, (t + 1) * w:(t + 2) * w]
    co_ref[:, (SSD_CONV - 2) * w:(SSD_CONV - 1) * w] = xn
    xact = _silu(conv)
    xs = xact[:, :SSD_INNER]

    dt = _softplus(dt_ref[...] + dtb_ref[...])
    decay = jnp.exp(dt * (-jnp.exp(alog_ref[...])))
    dtx = xs * _expand_heads(dt, e_ref)
    et = et_ref[...]
    d1, d2, d3 = _split3(decay)
    dcol = _dot_nt(et, d1) + _dot_nt(et, d2) + _dot_nt(et, d3)

    eye = (lax.broadcasted_iota(jnp.int32, (LANE, LANE), 0) == lax.broadcasted_iota(jnp.int32, (LANE, LANE), 1))
    hg = SSD_HEADS // SSD_GROUPS
    pairs_per_group = hg * SSD_HEAD_DIM // LANE
    rows = []
    for bb in range(DEC_BB):
        pieces = []
        for t in range(SSD_INNER // LANE):
            g = t // pairs_per_group
            ts = slice(t * LANE, (t + 1) * LANE)
            bo = SSD_INNER + g * SSD_STATE
            co = SSD_INNER + (SSD_GROUPS + g) * SSD_STATE
            bmat = jnp.broadcast_to(xact[bb:bb + 1, bo:bo + SSD_STATE], (LANE, SSD_STATE)).astype(BF16)
            upd = _dot(_row_diag(dtx[bb:bb + 1, ts], eye), bmat)
            s_new = dcol[ts, bb:bb + 1] * s_ref[bb, ts, :] + upd
            so_ref[bb, ts, :] = s_new
            c8 = jnp.broadcast_to(xact[bb:bb + 1, co:co + SSD_STATE], (8, SSD_STATE)).astype(BF16)
            pieces.append(_dot_nt(c8, s_new.astype(BF16))[0:1, :])
        rows.append(jnp.concatenate(pieces, axis=1))
    y = jnp.concatenate(rows, axis=0) + dskip_ref[...] * xs
    y_ref[...] = _group_rms_gate(y, z_ref[...], nw_ref[...])


def _ssd_dec(u, dt, p, conv_all, s_all, l, prev):
    bsz = u.shape[0]
    _, e = _ssd_consts()
    et = e.T
    st = _state_spec(l, (SSD_INNER, SSD_STATE))
    cst = _state_spec(l, (CONV_ST_W,))
    in_specs = [pl.BlockSpec((DEC_BB, SSD_CONV_DIM), lambda i: (i, OFF_XBC // SSD_CONV_DIM)),
                pl.BlockSpec((DEC_BB, SSD_INNER), lambda i: (i, OFF_Z // SSD_INNER)),
                pl.BlockSpec((DEC_BB, LANE), lambda i: (i, 0)),
                cst, *_ssd_param_specs(l),
                _const_spec((LANE, SSD_INNER)), _const_spec((SSD_INNER, LANE)), st]
    alias_specs, alias_args, aliases = _alias_args(l, len(in_specs), prev)
    return pl.pallas_call(
        _ssd_dec_kernel,
        grid=(bsz // DEC_BB,),
        in_specs=in_specs + alias_specs,
        out_specs=[pl.BlockSpec((DEC_BB, SSD_INNER), lambda i: (i, 0)), cst, st],
        out_shape=[jax.ShapeDtypeStruct((bsz, SSD_INNER), F32),
                   jax.ShapeDtypeStruct(conv_all.shape, F32),
                   jax.ShapeDtypeStruct(s_all.shape, F32)],
        input_output_aliases=aliases,
        compiler_params=_params(("parallel",)),
        name="ssd_dec",
    )(u, u, dt, conv_all, *_ssd_param_args(p), e, et, s_all, *alias_args)


def _per_head_lanes(x, swap):
    lane = lax.broadcasted_iota(jnp.int32, x.shape, 1)
    first = lane < SWA_HEAD_DIM
    a = jnp.where(first, x, swap)
    b = jnp.where(first, swap, x)
    return jnp.concatenate([a, a, b, b], axis=1)


def _swa_dec_kernel(rel_ref, sink_ref, q_ref, kn_ref, vn_ref, idx_ref, kc_ref, vc_ref, *rest, layer):
    o_ref, ko_ref, vo_ref = rest[-3:]
    bias = _build_bias(rel_ref, idx_ref[...], (8, 2 * LANE))
    rowi = lax.broadcasted_iota(jnp.int32, (SWA_Q_HEADS, 1), 0)
    bias_c = jnp.zeros((SWA_Q_HEADS, WINDOW), F32)
    bias_n = jnp.zeros((SWA_Q_HEADS, 1), F32)
    sink = jnp.zeros((SWA_Q_HEADS, 1), F32)
    for h in range(SWA_Q_HEADS):
        bias_c = jnp.where(rowi == h, bias[h][0:1, :WINDOW], bias_c)
        bias_n = jnp.where(rowi == h, bias[h][0:1, WINDOW:WINDOW + 1], bias_n)
        sink = jnp.where(rowi == h, sink_ref[layer, h], sink)
    own = (lax.broadcasted_iota(jnp.int32, (SWA_Q_HEADS, SWA_QW), 1) // SWA_HEAD_DIM
           == lax.broadcasted_iota(jnp.int32, (SWA_Q_HEADS, SWA_QW), 0))
    scale = SWA_HEAD_DIM ** -0.5
    q = q_ref[...]
    kn = kn_ref[...]
    vn = vn_ref[...]
    knx = _per_head_lanes(kn, pltpu.roll(kn, SWA_HEAD_DIM, 1))
    vnx = _per_head_lanes(vn, pltpu.roll(vn, SWA_HEAD_DIM, 1))
    rows = []
    for bb in range(DEC_BB):
        kc = kc_ref[bb]
        vc = vc_ref[bb]
        ko_ref[bb, 0:WINDOW - 1, :] = kc[1:WINDOW, :]
        ko_ref[bb, WINDOW - 1:WINDOW, :] = kn[bb:bb + 1, :]
        vo_ref[bb, 0:WINDOW - 1, :] = vc[1:WINDOW, :]
        vo_ref[bb, WINDOW - 1:WINDOW, :] = vn[bb:bb + 1, :]
        qx = jnp.where(own, jnp.broadcast_to(q[bb:bb + 1, :], (SWA_Q_HEADS, SWA_QW)), 0.0)
        kx = _per_head_lanes(kc, pltpu.roll(kc, SWA_HEAD_DIM, 1)).astype(BF16)
        vx = _per_head_lanes(vc, pltpu.roll(vc, SWA_HEAD_DIM, 1)).astype(BF16)
        qxb = qx.astype(BF16)
        s_c = _dot_nt(qxb, kx) * scale + bias_c
        knb = knx[bb:bb + 1, :].astype(BF16).astype(F32)
        s_n = jnp.sum(qxb.astype(F32) * knb, axis=-1, keepdims=True) * scale + bias_n
        m = jnp.maximum(jnp.maximum(jnp.max(s_c, axis=-1, keepdims=True), s_n), sink)
        e_c = jnp.exp(s_c - m)
        e_n = jnp.exp(s_n - m)
        den = jnp.sum(e_c, axis=-1, keepdims=True) + e_n + jnp.exp(sink - m)
        pv = _dot((e_c / den).astype(BF16), vx) + (e_n / den) * vnx[bb:bb + 1, :]
        rows.append(jnp.sum(jnp.where(own, pv, 0.0), axis=0, keepdims=True))
    o_ref[...] = jnp.concatenate(rows, axis=0)


def _swa_dec(u, rel_bias, sinks, k_all, v_all, l, prev):
    bsz = u.shape[0]
    dist = np.zeros((8, 2 * LANE), np.int64)
    dist[:, :WINDOW + 1] = WINDOW - np.arange(WINDOW + 1)[None, :]
    idx = jnp.asarray(_t5_bucket_np(dist))
    smem = pl.BlockSpec(memory_space=pltpu.SMEM)
    cache = _state_spec(l, (WINDOW, SWA_KW))
    in_specs = [smem, smem,
                pl.BlockSpec((DEC_BB, SWA_QW), lambda i: (i, OFF_QC // SWA_QW)),
                pl.BlockSpec((DEC_BB, SWA_KW), lambda i: (i, OFF_KC // SWA_KW)),
                pl.BlockSpec((DEC_BB, SWA_KW), lambda i: (i, OFF_VC // SWA_KW)),
                _const_spec((8, 2 * LANE)), cache, cache]
    alias_specs, alias_args, aliases = _alias_args(l, len(in_specs), prev)
    return pl.pallas_call(
        functools.partial(_swa_dec_kernel, layer=l),
        grid=(bsz // DEC_BB,),
        in_specs=in_specs + alias_specs,
        out_specs=[pl.BlockSpec((DEC_BB, SWA_QW), lambda i: (i, 0)), cache, cache],
        out_shape=[jax.ShapeDtypeStruct((bsz, SWA_QW), F32),
                   jax.ShapeDtypeStruct(k_all.shape, F32), jax.ShapeDtypeStruct(v_all.shape, F32)],
        input_output_aliases=aliases,
        compiler_params=_params(("parallel",)),
        name="swa_dec",
    )(rel_bias, sinks, u, u, u, idx, k_all, v_all, *alias_args)


def _rope_tables(pos):
    half = RET_DK // 2
    inv = ROPE_BASE ** (-jnp.arange(half, dtype=F32) / half)
    ang = pos[:, None] * inv[None, :]
    cos = jnp.cos(ang)
    sin = jnp.sin(ang)
    return jnp.concatenate([cos, cos], -1), jnp.concatenate([-sin, sin], -1)


W_IN_SIZES = (RET_W, RET_W, RET_W, RET_W, SSD_INNER, SSD_CONV_DIM, SSD_HEADS, SWA_QW, SWA_KW, SWA_KW, 3 * D_MODEL)
W_IN_ORDER = (10, 0, 1, 2, 3, 4, 5, 7, 8, 9, 6)
W_PREP_ROWS = 256


def _w_in_prep_kernel(w_ref, o_ref):
    src = [sum(W_IN_SIZES[:i]) for i in range(len(W_IN_SIZES))]
    off = 0
    for idx in W_IN_ORDER:
        width = W_IN_SIZES[idx]
        o_ref[:, off:off + width] = w_ref[:, src[idx]:src[idx] + width].astype(BF16)
        off += width
    o_ref[:, off:] = jnp.zeros((o_ref.shape[0], N_PROJ - off), BF16)


def _reorder_w_in(w):
    depth, k, n_in = w.shape
    return pl.pallas_call(
        _w_in_prep_kernel,
        grid=(depth, k // W_PREP_ROWS),
        in_specs=[pl.BlockSpec((None, W_PREP_ROWS, n_in), lambda l, i: (l, i, 0))],
        out_specs=pl.BlockSpec((None, W_PREP_ROWS, N_PROJ), lambda l, i: (l, i, 0)),
        out_shape=jax.ShapeDtypeStruct((depth, k, N_PROJ), BF16),
        compiler_params=_params(("parallel", "parallel")),
        name="w_in_prep",
    )(w)


def _prepare_params(w_in, conv_w, conv_b, dt_bias, a_log, d_skip, ssd_norm_w, sinks,
                    w_br_ret, w_br_ssd, w_br_swa, w_out, ln1_g, ln1_b, ln2_g, ln2_b,
                    w_ffn_gate, w_ffn_up, w_ffn_down):
    row = lambda v: v.astype(F32)[:, None, :]
    pad = lambda v: jnp.pad(v.astype(F32), ((0, 0), (0, LANE - v.shape[1])))[:, None, :]
    return {
        "w_in": _reorder_w_in(w_in),
        "conv_w": conv_w.astype(F32), "conv_b": row(conv_b),
        "dt_bias": pad(dt_bias), "a_log": pad(a_log),
        "d_skip": row(jnp.repeat(d_skip, SSD_HEAD_DIM, axis=1)),
        "norm_w": row(ssd_norm_w), "sinks": sinks.astype(F32),
        "w_br_ret": w_br_ret.astype(BF16), "w_br_ssd": w_br_ssd.astype(BF16),
        "w_br_swa": w_br_swa.astype(BF16), "w_out": w_out.astype(BF16),
        "ln1_g": row(ln1_g), "ln1_b": row(ln1_b), "ln2_g": row(ln2_g), "ln2_b": row(ln2_b),
        "w_ffn_gate": w_ffn_gate.astype(BF16), "w_ffn_up": w_ffn_up.astype(BF16),
        "w_ffn_down": w_ffn_down.astype(BF16),
    }


def _prompt_layer(x, xb, p, l, rel_bias, cos, sin, bsz, seq):
    m = bsz * seq
    u, dt = _in_proj(xb, p["w_in"], l, min(512, m), BF16)
    u3 = u.reshape(bsz, seq, N_U)
    o_r, s_ret = _ret_prompt(u3, cos, sin)
    y, s_ssm = _ssd_prompt(u3, dt.reshape(bsz, seq, LANE), p, l)
    o_c = _swa_prompt(u3, rel_bias, p["sinks"], l)
    x1, x1b = _merge(o_r.reshape(m, RET_W), y.reshape(m, SSD_INNER), o_c.reshape(m, SWA_QW), u, x, p, l,
                     min(512, m))
    x2, x2b = _ffn(x1b, x1, p, l, min(512, m))
    conv = u3[:, seq - (SSD_CONV - 1):, OFF_XBC:OFF_XBC + SSD_CONV_DIM].astype(F32)
    k_new = u3[:, seq - WINDOW:, OFF_KC:OFF_KC + SWA_KW].astype(F32)
    v_new = u3[:, seq - WINDOW:, OFF_VC:OFF_VC + SWA_KW].astype(F32)
    return x2, x2b, (s_ret, s_ssm.reshape(bsz, SSD_HEADS, SSD_HEAD_DIM, SSD_STATE), conv,
                     k_new.reshape(bsz, WINDOW, SWA_KV_HEADS, SWA_HEAD_DIM),
                     v_new.reshape(bsz, WINDOW, SWA_KV_HEADS, SWA_HEAD_DIM))


def _sample_layer(x, xb, p, l, rel_bias, cos, sin, states, prev):
    s_ret, s_ssm, s_conv, kbuf, vbuf = states
    bsz = x.shape[0]
    u, dt = _in_proj(xb, p["w_in"], l, bsz, F32)
    pr = prev if l else (None,) * 5
    o_r, ret_new = _ret_dec(u, cos, sin, s_ret, l, pr[0:1])
    y, conv_new, ssm_new = _ssd_dec(u, dt, p, s_conv, s_ssm, l, pr[1:3])
    o_c, k_new, v_new = _swa_dec(u, rel_bias, p["sinks"], kbuf, vbuf, l, pr[3:5])
    x1, x1b = _merge(o_r.astype(BF16), y.astype(BF16), o_c.astype(BF16), u, x, p, l, bsz)
    x2, x2b = _ffn(x1b, x1, p, l, bsz)
    return x2, x2b, (ret_new, conv_new, ssm_new, k_new, v_new)


def kernel(x_prompt, x_sample, state_ret, state_ssm, state_conv, cache_swa_k, cache_swa_v, w_in, conv_w, conv_b, dt_bias, a_log, d_skip, ssd_norm_w, sinks, rel_bias, w_br_ret, w_br_ssd, w_br_swa, w_out, ln1_g, ln1_b, ln2_g, ln2_b, w_ffn_gate, w_ffn_up, w_ffn_down):
    bsz, seq, _ = x_prompt.shape
    dbsz, dseq, _ = x_sample.shape
    assert seq % CHUNK == 0 and dseq == 1 and dbsz % DEC_BB == 0
    p = _prepare_params(w_in, conv_w, conv_b, dt_bias, a_log, d_skip, ssd_norm_w, sinks,
                        w_br_ret, w_br_ssd, w_br_swa, w_out, ln1_g, ln1_b, ln2_g, ln2_b,
                        w_ffn_gate, w_ffn_up, w_ffn_down)
    cos_p, sin_p = _rope_tables(jnp.arange(seq, dtype=F32))
    cos_s, sin_s = _rope_tables(PAST_LEN + jnp.arange(dseq, dtype=F32))
    rel = rel_bias.astype(F32)
    states = (state_ret.astype(F32),
              state_ssm.astype(F32).reshape(DEPTH, dbsz, SSD_INNER, SSD_STATE),
              state_conv.astype(F32).reshape(DEPTH, dbsz, CONV_ST_W),
              cache_swa_k.astype(F32).reshape(DEPTH, dbsz, WINDOW, SWA_KW),
              cache_swa_v.astype(F32).reshape(DEPTH, dbsz, WINDOW, SWA_KW))

    xp = x_prompt.reshape(bsz * seq, D_MODEL).astype(F32)
    xs = x_sample.reshape(dbsz * dseq, D_MODEL).astype(F32)
    xpb, xsb = xp, xs
    new_p = [[] for _ in range(5)]
    new_s = None
    for l in range(DEPTH):
        xp, xpb, st_p = _prompt_layer(xp, xpb, p, l, rel, cos_p, sin_p, bsz, seq)
        xs, xsb, (ret_new, conv_new, ssm_new, k_new, v_new) = _sample_layer(
            xs, xsb, p, l, rel, cos_s, sin_s, states, new_s)
        new_s = (ret_new, conv_new, ssm_new, k_new, v_new)
        for j in range(5):
            new_p[j].append(st_p[j])
    st_p = [jnp.stack(s) for s in new_p]
    ret_s, conv_s, ssm_s, k_s, v_s = new_s
    return (xp.reshape(bsz, seq, D_MODEL), xs.reshape(dbsz, dseq, D_MODEL), *st_p,
            ret_s, ssm_s.reshape(state_ssm.shape), conv_s.reshape(state_conv.shape),
            k_s.reshape(cache_swa_k.shape), v_s.reshape(cache_swa_v.shape))
```
